```python
import math
import jax, jax.numpy as jnp
from jax import lax
import numpy as np

D_MODEL = 1024
BATCH = 16
SEQ = 2048
DEPTH = 1
DEC_BATCH = 16
DEC_SEQ = 16
PAST_LEN = 2048

CHUNK = 64
M_HEADS = 4
M_HEAD_DIM = 128
M_WIDTH = M_HEADS * M_HEAD_DIM
CONV_WIDTH = 4
A_HEADS = 8
A_HEAD_DIM = 64
A_WIDTH = A_HEADS * A_HEAD_DIM
BAND_CHUNKS = 8
WINDOW = BAND_CHUNKS * CHUNK
REL_CLIP = 128
N_REL = 2 * REL_CLIP + 1
EPS = 1e-6
IN_WIDTHS = (M_WIDTH, M_WIDTH, M_WIDTH, M_WIDTH, M_WIDTH, M_HEADS, M_HEADS,
             A_WIDTH, A_WIDTH, A_WIDTH, A_WIDTH, D_MODEL, D_MODEL)
D_IN = sum(IN_WIDTHS)
F_GATE_OFFSET = sum(IN_WIDTHS[:6])

kernel_name = "hybrid_mlstm_chunkband_stream_step"


def rms_norm(x, g):
    xf = x.astype(jnp.float32)
    y = xf * lax.rsqrt(jnp.mean(xf * xf, axis=-1, keepdims=True) + EPS)
    return (y * g.astype(jnp.float32)).astype(x.dtype)


def in_projection(x, norm_g, w_in, b_in):
    h = rms_norm(x, norm_g)
    z = jnp.einsum("btd,de->bte", h, w_in) + b_in
    points = [int(p) for p in np.cumsum(IN_WIDTHS)[:-1]]
    return jnp.split(z, points, axis=-1)


def causal_conv(u, buf, w, b):
    T = u.shape[1]
    full = jnp.concatenate([buf.astype(u.dtype), u], axis=1)
    out = b + sum(full[:, j:j + T] * w[j] for j in range(CONV_WIDTH))
    return out, full[:, full.shape[1] - (CONV_WIDTH - 1):]


def mlstm_chunk(carry, inp):
    C, n, m = carry
    q, k, v, ig, fg = inp
    L = q.shape[1]
    b = jnp.cumsum(jax.nn.log_sigmoid(fg), axis=1)
    causal = jnp.tril(jnp.ones((L, L), dtype=bool))[None, :, :, None]
    logw = jnp.where(causal, b[:, :, None, :] - b[:, None, :, :] + ig[:, None, :, :], -jnp.inf)
    log_state = b + m[:, None, :]
    m_t = jnp.maximum(log_state, jnp.max(logw, axis=2))
    w_intra = jnp.exp(logw - m_t[:, :, None, :])
    w_state = jnp.exp(log_state - m_t)
    s = jnp.einsum("bthd,bshd->btsh", q, k) * w_intra
    num = jnp.einsum("btsh,bshd->bthd", s, v) + w_state[..., None] * jnp.einsum("bthk,bhkv->bthv", q, C)
    den = jnp.sum(s, axis=2) + w_state * jnp.einsum("bthk,bhk->bth", q, n)
    h = num / jnp.maximum(jnp.abs(den), jnp.exp(-m_t))[..., None]
    m_new = m_t[:, -1]
    w_end = jnp.exp(b[:, -1:] - b + ig - m_new[:, None])
    decay = jnp.exp(b[:, -1] + m - m_new)
    C_new = decay[..., None, None] * C + jnp.einsum("bsh,bshk,bshv->bhkv", w_end, k, v)
    n_new = decay[..., None] * n + jnp.einsum("bsh,bshk->bhk", w_end, k)
    return (C_new, n_new, m_new), h


def mlstm_branch(mq, mk, mv, mo, mi, mf, conv_buf, C0, n0, m0, conv_w, conv_b, m_head_g):
    B, T, _ = mq.shape
    f32 = jnp.float32
    qk, conv_new = causal_conv(jnp.concatenate([mq, mk], axis=-1), conv_buf, conv_w, conv_b)
    q, k = jnp.split(jax.nn.silu(qk), 2, axis=-1)
    q = q.reshape(B, T, M_HEADS, M_HEAD_DIM).astype(f32)
    k = k.reshape(B, T, M_HEADS, M_HEAD_DIM).astype(f32) * (M_HEAD_DIM ** -0.5)
    v = mv.reshape(B, T, M_HEADS, M_HEAD_DIM).astype(f32)
    ig = mi.astype(f32)
    fg = mf.astype(f32)
    nc = max(T // CHUNK, 1)
    L = T // nc

    def to_blocks(a):
        return jnp.moveaxis(a.reshape(B, nc, L, *a.shape[2:]), 1, 0)

    (C1, n1, m1), h = lax.scan(mlstm_chunk, (C0.astype(f32), n0.astype(f32), m0.astype(f32)),
                               (to_blocks(q), to_blocks(k), to_blocks(v), to_blocks(ig), to_blocks(fg)))
    h = jnp.moveaxis(h, 0, 1).reshape(B, T, M_HEADS, M_HEAD_DIM)
    h = rms_norm(h, m_head_g).reshape(B, T, M_WIDTH)
    h = (jax.nn.sigmoid(mo.astype(f32)) * h).astype(mq.dtype)
    return h, conv_new, C1.astype(C0.dtype), n1.astype(n0.dtype), m1.astype(m0.dtype)


def attn_heads(aq, ak, av, q_g, k_g):
    B, T, _ = aq.shape
    q = rms_norm(aq.reshape(B, T, A_HEADS, A_HEAD_DIM), q_g)
    k = rms_norm(ak.reshape(B, T, A_HEADS, A_HEAD_DIM), k_g)
    v = av.reshape(B, T, A_HEADS, A_HEAD_DIM)
    return q, k, v


def attend(q, k, v, dist, valid, rel_bias):
    bias = rel_bias[:, jnp.clip(dist, -REL_CLIP, REL_CLIP) + REL_CLIP].astype(jnp.float32)
    s = jnp.einsum("bqhd,bkhd->bhqk", q, k, preferred_element_type=jnp.float32) * (A_HEAD_DIM ** -0.5) + bias
    if valid is not None:
        s = jnp.where(valid, s, -jnp.inf)
    p = jax.nn.softmax(s, axis=-1)
    return jnp.einsum("bhqk,bkhd->bqhd", p.astype(v.dtype), v)


def band_attention_prompt(q, k, v, rel_bias):
    B, T, H, D = q.shape
    nc = T // CHUNK
    band = WINDOW + CHUNK
    kp = jnp.pad(k, ((0, 0), (WINDOW, 0), (0, 0), (0, 0)))
    vp = jnp.pad(v, ((0, 0), (WINDOW, 0), (0, 0), (0, 0)))
    qc = q.reshape(B, nc, CHUNK, H, D)
    key_idx = jnp.arange(band)
    dist = jnp.arange(CHUNK)[:, None] + WINDOW - key_idx[None, :]

    def one_chunk(c):
        kb = lax.dynamic_slice_in_dim(kp, c * CHUNK, band, axis=1)
        vb = lax.dynamic_slice_in_dim(vp, c * CHUNK, band, axis=1)
        qb = lax.dynamic_index_in_dim(qc, c, axis=1, keepdims=False)
        valid = (key_idx >= WINDOW - c * CHUNK)[None, :]
        return attend(qb, kb, vb, dist, valid, rel_bias)

    out = lax.map(one_chunk, jnp.arange(nc))
    return jnp.moveaxis(out, 0, 1).reshape(B, T, H * D)


def band_attention_cached(q, k, v, k_past, v_past, rel_bias):
    B, T, H, D = q.shape
    Lc = k_past.shape[1]
    kb = jnp.concatenate([k_past.astype(k.dtype), k], axis=1)
    vb = jnp.concatenate([v_past.astype(v.dtype), v], axis=1)
    dist = jnp.arange(T)[:, None] + Lc - jnp.arange(Lc + T)[None, :]
    return attend(q, kb, vb, dist, None, rel_bias).reshape(B, T, H * D)


def merge_out(x, h_m, mz, h_a, az, gm, ga, w_bm, w_ba, w_out):
    u_m = jnp.einsum("btc,cd->btd", h_m * jax.nn.silu(mz), w_bm)
    u_a = jnp.einsum("btc,cd->btd", h_a * jax.nn.silu(az), w_ba)
    mix = jax.nn.sigmoid(gm) * u_m + jax.nn.sigmoid(ga) * u_a
    return x + jnp.einsum("btd,de->bte", mix, w_out)


def setup_inputs(seed: int = 0) -> dict:
    key = jax.random.key(seed)
    ks = jax.random.split(key, 20)
    f32 = jnp.float32

    def nrm(k, shape, scale):
        return scale * jax.random.normal(k, shape, f32)

    lc = min(WINDOW, PAST_LEN)
    b_in = nrm(ks[10], (DEPTH, D_IN), 0.01)
    b_in = b_in.at[:, F_GATE_OFFSET:F_GATE_OFFSET + M_HEADS].add(jnp.linspace(3.0, 6.0, M_HEADS, dtype=f32))
    return {
        "x_prompt": nrm(ks[0], (BATCH, SEQ, D_MODEL), 1.0),
        "x_sample": nrm(ks[1], (DEC_BATCH, DEC_SEQ, D_MODEL), 1.0),
        "state_mlstm_C": nrm(ks[2], (DEPTH, DEC_BATCH, M_HEADS, M_HEAD_DIM, M_HEAD_DIM), 0.1),
        "state_mlstm_n": nrm(ks[3], (DEPTH, DEC_BATCH, M_HEADS, M_HEAD_DIM), 0.1),
        "state_mlstm_m": nrm(ks[4], (DEPTH, DEC_BATCH, M_HEADS), 1.0),
        "state_mlstm_conv": nrm(ks[5], (DEPTH, DEC_BATCH, CONV_WIDTH - 1, 2 * M_WIDTH), 1.0),
        "cache_attn_k": nrm(ks[6], (DEPTH, DEC_BATCH, lc, A_HEADS, A_HEAD_DIM), 1.0),
        "cache_attn_v": nrm(ks[7], (DEPTH, DEC_BATCH, lc, A_HEADS, A_HEAD_DIM), 1.0),
        "norm_g": 1.0 + nrm(ks[8], (DEPTH, D_MODEL), 0.02),
        "w_in": nrm(ks[9], (DEPTH, D_MODEL, D_IN), D_MODEL ** -0.5),
        "b_in": b_in,
        "conv_w": nrm(ks[11], (DEPTH, CONV_WIDTH, 2 * M_WIDTH), CONV_WIDTH ** -0.5),
        "conv_b": nrm(ks[12], (DEPTH, 2 * M_WIDTH), 0.01),
        "m_head_g": 1.0 + nrm(ks[13], (DEPTH, M_HEADS, M_HEAD_DIM), 0.02),
        "q_norm_g": 1.0 + nrm(ks[14], (DEPTH, A_HEAD_DIM), 0.02),
        "k_norm_g": 1.0 + nrm(ks[15], (DEPTH, A_HEAD_DIM), 0.02),
        "rel_bias": nrm(ks[16], (DEPTH, A_HEADS, N_REL), 0.1),
        "w_bm": nrm(ks[17], (DEPTH, M_WIDTH, D_MODEL), M_WIDTH ** -0.5),
        "w_ba": nrm(ks[18], (DEPTH, A_WIDTH, D_MODEL), A_WIDTH ** -0.5),
        "w_out": nrm(ks[19], (DEPTH, D_MODEL, D_MODEL), D_MODEL ** -0.5),
    }


def reference(x_prompt, x_sample, state_mlstm_C, state_mlstm_n, state_mlstm_m, state_mlstm_conv,
              cache_attn_k, cache_attn_v, norm_g, w_in, b_in, conv_w, conv_b, m_head_g,
              q_norm_g, k_norm_g, rel_bias, w_bm, w_ba, w_out):
    xp, xs = x_prompt, x_sample
    Bp, Tp, _ = xp.shape
    keep = min(WINDOW, Tp)
    pC, pn, pm, pconv, pk, pv = [], [], [], [], [], []
    sC, sn, sm, sconv, sk, sv = [], [], [], [], [], []
    for l in range(DEPTH):
        mq, mk, mv, mo, mz, mi, mf, aq, ak, av, az, gm, ga = in_projection(xp, norm_g[l], w_in[l], b_in[l])
        conv0 = jnp.zeros((Bp, CONV_WIDTH - 1, 2 * M_WIDTH), xp.dtype)
        C0 = jnp.zeros((Bp, M_HEADS, M_HEAD_DIM, M_HEAD_DIM), jnp.float32)
        n0 = jnp.zeros((Bp, M_HEADS, M_HEAD_DIM), jnp.float32)
        m0 = jnp.zeros((Bp, M_HEADS), jnp.float32)
        h_m, conv_p, C_p, n_p, m_p = mlstm_branch(mq, mk, mv, mo, mi, mf, conv0, C0, n0, m0,
                                                  conv_w[l], conv_b[l], m_head_g[l])
        q, k, v = attn_heads(aq, ak, av, q_norm_g[l], k_norm_g[l])
        h_a = band_attention_prompt(q, k, v, rel_bias[l])
        xp = merge_out(xp, h_m, mz, h_a, az, gm, ga, w_bm[l], w_ba[l], w_out[l])
        pC.append(C_p); pn.append(n_p); pm.append(m_p); pconv.append(conv_p)
        pk.append(k[:, Tp - keep:]); pv.append(v[:, Tp - keep:])

        mq, mk, mv, mo, mz, mi, mf, aq, ak, av, az, gm, ga = in_projection(xs, norm_g[l], w_in[l], b_in[l])
        h_m, conv_s, C_s, n_s, m_s = mlstm_branch(mq, mk, mv, mo, mi, mf, state_mlstm_conv[l],
                                                  state_mlstm_C[l], state_mlstm_n[l], state_mlstm_m[l],
                                                  conv_w[l], conv_b[l], m_head_g[l])
        q, k, v = attn_heads(aq, ak, av, q_norm_g[l], k_norm_g[l])
        h_a = band_attention_cached(q, k, v, cache_attn_k[l], cache_attn_v[l], rel_bias[l])
        xs = merge_out(xs, h_m, mz, h_a, az, gm, ga, w_bm[l], w_ba[l], w_out[l])
        sC.append(C_s); sn.append(n_s); sm.append(m_s); sconv.append(conv_s)
        sk.append(k); sv.append(v)

    return (xp, xs,
            jnp.stack(pC), jnp.stack(pn), jnp.stack(pm), jnp.stack(pconv), jnp.stack(pk), jnp.stack(pv),
            jnp.stack(sC), jnp.stack(sn), jnp.stack(sm), jnp.stack(sconv), jnp.stack(sk), jnp.stack(sv))
```

```python
import functools

import jax
import jax.numpy as jnp
import numpy as np
from jax import lax
from jax.experimental import pallas as pl
from jax.experimental.pallas import tpu as pltpu

D_MODEL = 1024
CHUNK = 64
M_HEADS = 4
M_HEAD_DIM = 128
M_WIDTH = M_HEADS * M_HEAD_DIM
CONV_WIDTH = 4
A_HEADS = 8
A_HEAD_DIM = 64
A_WIDTH = A_HEADS * A_HEAD_DIM
A_PAIRS = A_HEADS // 2
WINDOW = 8 * CHUNK
REL_CLIP = 128
EPS = 1e-6
IN_WIDTHS = (M_WIDTH, M_WIDTH, M_WIDTH, M_WIDTH, M_WIDTH, M_HEADS, M_HEADS,
             A_WIDTH, A_WIDTH, A_WIDTH, A_WIDTH, D_MODEL, D_MODEL)

LANES = 128
SUBLANES = 8
VMEM_LIMIT_BYTES = 56 * 1024 * 1024

Z_MV = 0
Z_MO = Z_MV + M_WIDTH
Z_MZ = Z_MO + M_WIDTH
Z_AQ = Z_MZ + M_WIDTH
Z_AK = Z_AQ + A_WIDTH
Z_AV = Z_AK + A_WIDTH
Z_AZ = Z_AV + A_WIDTH
Z_GM = Z_AZ + A_WIDTH
Z_GA = Z_GM + D_MODEL
Z_GATE = Z_GA + D_MODEL
Z_WIDTH = Z_GATE + LANES
QK_WIDTH = 2 * M_WIDTH
PROJ_BLOCK = 512

_NT = (((1,), (1,)), ((), ()))
_TN = (((0,), (0,)), ((), ()))


def _sigmoid(x):
    return 1.0 / (1.0 + jnp.exp(-x))


def _silu(x):
    return x * _sigmoid(x)


def _log_sigmoid(x):
    return jnp.minimum(x, 0.0) - jnp.log1p(jnp.exp(-jnp.abs(x)))


def _bf16(x):
    return x.astype(jnp.bfloat16)


def _dot(a, b):
    return jnp.dot(a, b, preferred_element_type=jnp.float32)


def _layer_kernel(*refs, tile, chunk, n_tiles, has_state):
    n_chunks = tile // chunk
    band = WINDOW + chunk
    it = iter(refs)
    x_ref = next(it)
    if has_state:
        c0_ref, n0_ref, m0_ref, conv0_ref, k0_ref, v0_ref = (next(it) for _ in range(6))
    (ng_ref, w_ref, b_ref, wgt_ref, bgt_ref, cw_ref, cb_ref, mg_ref, qg_ref, kg_ref,
     bias_ref, wbm_ref, wba_ref, wout_ref) = (next(it) for _ in range(14))
    y_ref, c_ref, n_out_ref, m_out_ref, conv_out_ref, k_out_ref, v_out_ref = (next(it) for _ in range(7))
    (hb_ref, u_ref, z_ref, qs_ref, ks_ref, kh_ref, vh_ref, gmix_ref, gatt_ref, mix_ref,
     n_ref, m_ref) = (next(it) for _ in range(12))

    t = pl.program_id(1)

    @pl.when(t == 0)
    def _init():
        if has_state:
            c_ref[0] = c0_ref[0]
            n_ref[0:M_HEADS, :] = n0_ref[0]
            m_ref[0:M_HEADS, :] = m0_ref[0]
            u_ref[0:SUBLANES, :] = conv0_ref[0]
            kh_ref[0:WINDOW, :] = _bf16(k0_ref[0])
            vh_ref[0:WINDOW, :] = _bf16(v0_ref[0])
        else:
            c_ref[...] = jnp.zeros_like(c_ref)
            n_ref[...] = jnp.zeros_like(n_ref)
            m_ref[...] = jnp.zeros_like(m_ref)
            u_ref[0:SUBLANES, :] = jnp.zeros((SUBLANES, QK_WIDTH), jnp.float32)
            kh_ref[0:WINDOW, :] = jnp.zeros((WINDOW, A_WIDTH), jnp.bfloat16)
            vh_ref[0:WINDOW, :] = jnp.zeros((WINDOW, A_WIDTH), jnp.bfloat16)

    x = x_ref[0]
    inv = lax.rsqrt(jnp.mean(x * x, axis=-1, keepdims=True) + EPS)
    hb_ref[...] = _bf16(x * inv * ng_ref[...])
    for j in range(0, QK_WIDTH, PROJ_BLOCK):
        u_ref[SUBLANES:SUBLANES + tile, j:j + PROJ_BLOCK] = (
            _dot(hb_ref[...], w_ref[:, j:j + PROJ_BLOCK]) + b_ref[:, j:j + PROJ_BLOCK])
    for j in range(0, Z_WIDTH, PROJ_BLOCK):
        wd = min(PROJ_BLOCK, Z_WIDTH - j)
        z_ref[:, j:j + wd] = (_dot(hb_ref[...], w_ref[:, QK_WIDTH + j:QK_WIDTH + j + wd])
                              + b_ref[:, QK_WIDTH + j:QK_WIDTH + j + wd])
    g_rows = lax.dot_general(wgt_ref[...], hb_ref[...], _NT,
                             preferred_element_type=jnp.float32) + bgt_ref[...]

    for j in range(0, QK_WIDTH, LANES):
        cs = slice(j, j + LANES)
        acc = cb_ref[:, cs] + cw_ref[CONV_WIDTH - 1:CONV_WIDTH, cs] * u_ref[SUBLANES:SUBLANES + tile, cs]
        for d in range(1, CONV_WIDTH):
            acc = acc + (cw_ref[CONV_WIDTH - 1 - d:CONV_WIDTH - d, cs]
                         * u_ref[SUBLANES - d:SUBLANES - d + tile, cs])
        act = _silu(acc)
        if j < M_WIDTH:
            qs_ref[:, cs] = act
        else:
            ks_ref[:, j - M_WIDTH:j - M_WIDTH + LANES] = act * (M_HEAD_DIM ** -0.5)

    @pl.when(t == n_tiles - 1)
    def _conv_out():
        conv_out_ref[0] = u_ref[tile:tile + SUBLANES, :]

    if n_tiles > 1:
        u_ref[0:SUBLANES, :] = u_ref[tile:tile + SUBLANES, :]

    row = lax.broadcasted_iota(jnp.int32, (chunk, chunk), 0)
    col = lax.broadcasted_iota(jnp.int32, (chunk, chunk), 1)
    lower = row >= col
    upper = row <= col
    neg_inf = jnp.float32(-jnp.inf)
    for c in range(n_chunks):
        rs = slice(c * chunk, (c + 1) * chunk)
        for h in range(M_HEADS):
            cs = slice(h * M_HEAD_DIM, (h + 1) * M_HEAD_DIM)
            q = qs_ref[rs, cs]
            k = ks_ref[rs, cs]
            v = z_ref[rs, Z_MV + h * M_HEAD_DIM:Z_MV + (h + 1) * M_HEAD_DIM]
            ig_row = g_rows[h:h + 1, rs]
            fg_row = g_rows[M_HEADS + h:M_HEADS + h + 1, rs]
            ig_col = z_ref[rs, Z_GATE + h:Z_GATE + h + 1]
            fg_col = z_ref[rs, Z_GATE + M_HEADS + h:Z_GATE + M_HEADS + h + 1]
            lf_row = _log_sigmoid(fg_row)
            lf_col = _log_sigmoid(fg_col)
            b_col = jnp.sum(jnp.where(lower, lf_row, 0.0), axis=1, keepdims=True)
            b_row = jnp.sum(jnp.where(upper, lf_col, 0.0), axis=0, keepdims=True)
            a_row = ig_row - b_row
            a_col = ig_col - b_col
            m_prev = m_ref[h:h + 1, 0:1]
            m_run = jnp.maximum(jnp.max(jnp.where(lower, a_row, neg_inf), axis=1, keepdims=True), m_prev)
            decay_intra = jnp.where(lower, jnp.exp(a_row - m_run), 0.0)
            w_state = jnp.exp(m_prev - m_run)
            m_t = b_col + m_run
            qb, kb, vb = _bf16(q), _bf16(k), _bf16(v)
            s = lax.dot_general(qb, kb, _NT, preferred_element_type=jnp.float32) * decay_intra
            c_prev = c_ref[0, h]
            n_prev = n_ref[h:h + 1, :]
            num = _dot(_bf16(s), vb) + w_state * _dot(qb, _bf16(c_prev))
            den = (jnp.sum(s, axis=1, keepdims=True)
                   + w_state * jnp.sum(q * n_prev, axis=1, keepdims=True))
            hh = num / jnp.maximum(jnp.abs(den), jnp.exp(-m_t))
            m_last = m_run[chunk - 1:chunk, :]
            w_end = jnp.exp(a_col - m_last)
            decay = jnp.exp(m_prev - m_last)
            kw = k * w_end
            c_ref[0, h] = decay * c_prev + lax.dot_general(_bf16(kw), vb, _TN,
                                                           preferred_element_type=jnp.float32)
            n_ref[h:h + 1, :] = decay * n_prev + jnp.sum(kw, axis=0, keepdims=True)
            m_ref[h:h + 1, :] = jnp.broadcast_to(m_t[chunk - 1:chunk, :], (1, LANES))
            hn = hh * lax.rsqrt(jnp.mean(hh * hh, axis=1, keepdims=True) + EPS) * mg_ref[h:h + 1, :]
            hm = _sigmoid(z_ref[rs, Z_MO + h * M_HEAD_DIM:Z_MO + (h + 1) * M_HEAD_DIM]) * hn
            gmix_ref[rs, cs] = _bf16(hm * _silu(z_ref[rs, Z_MZ + h * M_HEAD_DIM:Z_MZ + (h + 1) * M_HEAD_DIM]))

    @pl.when(t == n_tiles - 1)
    def _state_out():
        n_out_ref[0] = n_ref[0:M_HEADS, :]
        m_out_ref[0] = m_ref[0:M_HEADS, :]

    lo = lax.broadcasted_iota(jnp.int32, (1, LANES), 1) < A_HEAD_DIM

    def head_norm(a, g):
        sq = a * a
        s_lo = jnp.sum(jnp.where(lo, sq, 0.0), axis=1, keepdims=True)
        s_hi = jnp.sum(jnp.where(lo, 0.0, sq), axis=1, keepdims=True)
        ms = jnp.where(lo, s_lo, s_hi) * (1.0 / A_HEAD_DIM)
        return a * lax.rsqrt(ms + EPS) * g

    if n_tiles > 1:
        ring = pl.multiple_of((t * tile) % WINDOW, tile)
    for p in range(A_PAIRS):
        cs = slice(p * LANES, (p + 1) * LANES)
        kn = head_norm(z_ref[:, Z_AK + p * LANES:Z_AK + (p + 1) * LANES], kg_ref[...])
        vv = z_ref[:, Z_AV + p * LANES:Z_AV + (p + 1) * LANES]
        kh_ref[WINDOW:WINDOW + tile, cs] = _bf16(kn)
        vh_ref[WINDOW:WINDOW + tile, cs] = _bf16(vv)
        if n_tiles > 1:
            k_out_ref[0, pl.ds(ring, tile), cs] = kn
            v_out_ref[0, pl.ds(ring, tile), cs] = vv
        else:
            k_out_ref[0, :, cs] = kn
            v_out_ref[0, :, cs] = vv

    kcol = lax.broadcasted_iota(jnp.int32, (1, band), 1)
    for c in range(n_chunks):
        rs = slice(c * chunk, (c + 1) * chunk)
        bs = slice(c * chunk, c * chunk + band)
        for p in range(A_PAIRS):
            cs = slice(p * LANES, (p + 1) * LANES)
            qn = head_norm(z_ref[rs, Z_AQ + p * LANES:Z_AQ + (p + 1) * LANES], qg_ref[...])
            qn = qn * (A_HEAD_DIM ** -0.5)
            q2 = _bf16(jnp.concatenate([jnp.where(lo, qn, 0.0), jnp.where(lo, 0.0, qn)], axis=0))
            s = lax.dot_general(q2, kh_ref[bs, cs], _NT, preferred_element_type=jnp.float32) + bias_ref[p]
            if not has_state:
                first_valid = WINDOW - t * tile - c * chunk
                s = jnp.where(kcol >= first_valid, s, neg_inf)
            e = jnp.exp(s - jnp.max(s, axis=1, keepdims=True))
            o2 = _dot(_bf16(e), vh_ref[bs, cs]) / jnp.sum(e, axis=1, keepdims=True)
            o = jnp.where(lo, o2[0:chunk], o2[chunk:2 * chunk])
            gatt_ref[rs, cs] = _bf16(o * _silu(z_ref[rs, Z_AZ + p * LANES:Z_AZ + (p + 1) * LANES]))

    if n_tiles > 1:
        for r in range(0, WINDOW, LANES):
            kh_ref[r:r + LANES, :] = kh_ref[tile + r:tile + r + LANES, :]
            vh_ref[r:r + LANES, :] = vh_ref[tile + r:tile + r + LANES, :]

    for j in range(0, D_MODEL, PROJ_BLOCK):
        cs = slice(j, j + PROJ_BLOCK)
        u_m = _dot(gmix_ref[...], wbm_ref[:, cs])
        u_a = _dot(gatt_ref[...], wba_ref[:, cs])
        mix_ref[:, cs] = _bf16(_sigmoid(z_ref[:, Z_GM + j:Z_GM + j + PROJ_BLOCK]) * u_m
                               + _sigmoid(z_ref[:, Z_GA + j:Z_GA + j + PROJ_BLOCK]) * u_a)
    for j in range(0, D_MODEL, PROJ_BLOCK):
        cs = slice(j, j + PROJ_BLOCK)
        y_ref[0, :, cs] = x_ref[0, :, cs] + _dot(mix_ref[...], wout_ref[:, cs])


def _const_spec(shape):
    zeros = (0,) * len(shape)
    return pl.BlockSpec(shape, lambda b, t: zeros, pipeline_mode=pl.Buffered(1))


def _layer_call(x, state, params, *, tile, chunk):
    n_streams, seq, _ = x.shape
    n_tiles = seq // tile
    has_state = state is not None
    keep = min(WINDOW, seq)
    band = WINDOW + chunk

    def per_stream(shape):
        nd = len(shape)
        return pl.BlockSpec((1,) + tuple(shape[1:]), lambda b, t: (b,) + (0,) * (nd - 1))

    in_specs = [pl.BlockSpec((1, tile, D_MODEL), lambda b, t: (b, t, 0))]
    args = [x]
    if has_state:
        for a in state:
            in_specs.append(per_stream(a.shape))
            args.append(a)
    for a in params:
        in_specs.append(_const_spec(a.shape))
        args.append(a)

    f32 = jnp.float32
    out_shape = (
        jax.ShapeDtypeStruct((n_streams, seq, D_MODEL), f32),
        jax.ShapeDtypeStruct((n_streams, M_HEADS, M_HEAD_DIM, M_HEAD_DIM), f32),
        jax.ShapeDtypeStruct((n_streams, M_HEADS, LANES), f32),
        jax.ShapeDtypeStruct((n_streams, M_HEADS, LANES), f32),
        jax.ShapeDtypeStruct((n_streams, SUBLANES, QK_WIDTH), f32),
        jax.ShapeDtypeStruct((n_streams, keep, A_WIDTH), f32),
        jax.ShapeDtypeStruct((n_streams, keep, A_WIDTH), f32),
    )
    out_specs = (pl.BlockSpec((1, tile, D_MODEL), lambda b, t: (b, t, 0)),) + tuple(
        per_stream(s.shape) for s in out_shape[1:])

    bf16 = jnp.bfloat16
    scratch = [
        pltpu.VMEM((tile, D_MODEL), bf16),
        pltpu.VMEM((tile + SUBLANES, QK_WIDTH), f32),
        pltpu.VMEM((tile, Z_WIDTH), f32),
        pltpu.VMEM((tile, M_WIDTH), f32),
        pltpu.VMEM((tile, M_WIDTH), f32),
        pltpu.VMEM((WINDOW + tile, A_WIDTH), bf16),
        pltpu.VMEM((WINDOW + tile, A_WIDTH), bf16),
        pltpu.VMEM((tile, M_WIDTH), bf16),
        pltpu.VMEM((tile, A_WIDTH), bf16),
        pltpu.VMEM((tile, D_MODEL), bf16),
        pltpu.VMEM((SUBLANES, LANES), f32),
        pltpu.VMEM((SUBLANES, LANES), f32),
    ]
    kern = functools.partial(_layer_kernel, tile=tile, chunk=chunk, n_tiles=n_tiles, has_state=has_state)
    return pl.pallas_call(
        kern,
        grid=(n_streams, n_tiles),
        in_specs=in_specs,
        out_specs=out_specs,
        out_shape=out_shape,
        scratch_shapes=scratch,
        compiler_params=pltpu.CompilerParams(
            dimension_semantics=("arbitrary", "arbitrary"),
            vmem_limit_bytes=VMEM_LIMIT_BYTES),
        name="layer_state" if has_state else "layer_fresh",
    )(*args)


def _pair_bias(rel_bias, chunk):
    band = WINDOW + chunk
    dist = np.arange(chunk)[:, None] + WINDOW - np.arange(band)[None, :]
    idx = np.clip(dist, -REL_CLIP, REL_CLIP) + REL_CLIP
    bias = rel_bias[:, idx].astype(jnp.float32)
    return bias.reshape(A_PAIRS, 2 * chunk, band)


def _layer_params(norm_g, w_in, b_in, conv_w, conv_b, m_head_g, q_norm_g, k_norm_g, rel_bias,
                  w_bm, w_ba, w_out, chunk):
    points = [int(p) for p in np.cumsum(IN_WIDTHS)[:-1]]
    wq, wk, wv, wo, wz, wi, wf, waq, wak, wav, waz, wgm, wga = jnp.split(w_in, points, axis=1)
    bq, bk, bv, bo, bz, bi, bf, baq, bak, bav, baz, bgm, bga = jnp.split(b_in, points)
    w_gate = jnp.concatenate([wi, wf], axis=1)
    b_gate = jnp.concatenate([bi, bf])
    pad = LANES - 2 * M_HEADS
    w_main = jnp.concatenate([wq, wk, wv, wo, wz, waq, wak, wav, waz, wgm, wga,
                              jnp.pad(w_gate, ((0, 0), (0, pad)))], axis=1).astype(jnp.bfloat16)
    b_main = jnp.concatenate([bq, bk, bv, bo, bz, baq, bak, bav, baz, bgm, bga,
                              jnp.pad(b_gate, (0, pad))])[None, :]
    return (norm_g[None, :], w_main, b_main, w_gate.T.astype(jnp.bfloat16), b_gate[:, None],
            conv_w, conv_b[None, :], m_head_g, jnp.tile(q_norm_g, 2)[None, :], jnp.tile(k_norm_g, 2)[None, :],
            _pair_bias(rel_bias, chunk), w_bm.astype(jnp.bfloat16), w_ba.astype(jnp.bfloat16),
            w_out.astype(jnp.bfloat16))


PROMPT_TILE = 256


def kernel(x_prompt, x_sample, state_mlstm_C, state_mlstm_n, state_mlstm_m, state_mlstm_conv,
           cache_attn_k, cache_attn_v, norm_g, w_in, b_in, conv_w, conv_b, m_head_g,
           q_norm_g, k_norm_g, rel_bias, w_bm, w_ba, w_out):
    depth = w_in.shape[0]
    xp, xs = x_prompt, x_sample
    n_p, t_p, _ = xp.shape
    n_s, t_s, _ = xs.shape
    outs_p, outs_s = [], []
    for l in range(depth):
        weights = (norm_g[l], w_in[l], b_in[l], conv_w[l], conv_b[l], m_head_g[l], q_norm_g[l],
                   k_norm_g[l], rel_bias[l], w_bm[l], w_ba[l], w_out[l])
        xp, c_p, n_p_, m_p, conv_p, k_p, v_p = _layer_call(
            xp, None, _layer_params(*weights, chunk=CHUNK), tile=PROMPT_TILE, chunk=CHUNK)
        outs_p.append((c_p, n_p_, m_p[:, :, 0], conv_p[:, SUBLANES - (CONV_WIDTH - 1):],
                       k_p.reshape(n_p, -1, A_HEADS, A_HEAD_DIM), v_p.reshape(n_p, -1, A_HEADS, A_HEAD_DIM)))
        state = (state_mlstm_C[l], state_mlstm_n[l],
                 jnp.broadcast_to(state_mlstm_m[l][:, :, None], (n_s, M_HEADS, LANES)),
                 jnp.pad(state_mlstm_conv[l], ((0, 0), (SUBLANES - (CONV_WIDTH - 1), 0), (0, 0))),
                 cache_attn_k[l].reshape(n_s, -1, A_WIDTH), cache_attn_v[l].reshape(n_s, -1, A_WIDTH))
        xs, c_s, n_s_, m_s, conv_s, k_s, v_s = _layer_call(
            xs, state, _layer_params(*weights, chunk=t_s), tile=t_s, chunk=t_s)
        outs_s.append((c_s, n_s_, m_s[:, :, 0], conv_s[:, SUBLANES - (CONV_WIDTH - 1):],
                       k_s.reshape(n_s, -1, A_HEADS, A_HEAD_DIM), v_s.reshape(n_s, -1, A_HEADS, A_HEAD_DIM)))
    stack = lambda outs, i: jnp.stack([o[i] for o in outs])
    return (xp, xs) + tuple(stack(outs_p, i) for i in range(6)) + tuple(stack(outs_s, i) for i in range(6))
```

```python
import functools

import jax
import jax.numpy as jnp
import numpy as np
from jax import lax
from jax.experimental import pallas as pl
from jax.experimental.pallas import tpu as pltpu

D_MODEL = 1024
CHUNK = 64
M_HEADS = 4
M_HEAD_DIM = 128
M_WIDTH = M_HEADS * M_HEAD_DIM
CONV_WIDTH = 4
A_HEADS = 8
A_HEAD_DIM = 64
A_WIDTH = A_HEADS * A_HEAD_DIM
A_PAIRS = A_HEADS // 2
WINDOW = 8 * CHUNK
REL_CLIP = 128
EPS = 1e-6
IN_WIDTHS = (M_WIDTH, M_WIDTH, M_WIDTH, M_WIDTH, M_WIDTH, M_HEADS, M_HEADS,
             A_WIDTH, A_WIDTH, A_WIDTH, A_WIDTH, D_MODEL, D_MODEL)

LANES = 128
SUBLANES = 8
VMEM_LIMIT_BYTES = 56 * 1024 * 1024

Z_MV = 0
Z_MO = Z_MV + M_WIDTH
Z_MZ = Z_MO + M_WIDTH
Z_AQ = Z_MZ + M_WIDTH
Z_AK = Z_AQ + A_WIDTH
Z_AV = Z_AK + A_WIDTH
Z_AZ = Z_AV + A_WIDTH
Z_GM = Z_AZ + A_WIDTH
Z_GA = Z_GM + D_MODEL
Z_GATE = Z_GA + D_MODEL
Z_WIDTH = Z_GATE + LANES
QK_WIDTH = 2 * M_WIDTH
PROJ_BLOCK = 512

_NT = (((1,), (1,)), ((), ()))
_TN = (((0,), (0,)), ((), ()))


def _sigmoid(x):
    return 1.0 / (1.0 + jnp.exp(-x))


def _silu(x):
    return x * _sigmoid(x)


def _log_sigmoid(x):
    return jnp.minimum(x, 0.0) - jnp.log1p(jnp.exp(-jnp.abs(x)))


def _bf16(x):
    return x.astype(jnp.bfloat16)


def _dot(a, b):
    return jnp.dot(a, b, preferred_element_type=jnp.float32)


def _layer_kernel(*refs, tile, chunk, n_tiles, has_state):
    n_chunks = tile // chunk
    band = WINDOW + chunk
    it = iter(refs)
    x_ref = next(it)
    if has_state:
        c0_ref, n0_ref, m0_ref, conv0_ref, k0_ref, v0_ref = (next(it) for _ in range(6))
    (ng_ref, w_ref, b_ref, wgt_ref, bgt_ref, cw_ref, cb_ref, mg_ref, qg_ref, kg_ref,
     bias_ref, wbm_ref, wba_ref, wout_ref) = (next(it) for _ in range(14))
    y_ref, c_ref, n_out_ref, m_out_ref, conv_out_ref, k_out_ref, v_out_ref = (next(it) for _ in range(7))
    (hb_ref, u_ref, z_ref, qs_ref, ks_ref, kh_ref, vh_ref, gmix_ref, gatt_ref, mix_ref,
     n_ref, m_ref) = (next(it) for _ in range(12))

    t = pl.program_id(1)

    @pl.when(t == 0)
    def _init():
        if has_state:
            c_ref[0] = c0_ref[0]
            n_ref[0:M_HEADS, :] = n0_ref[0]
            m_ref[0:M_HEADS, :] = m0_ref[0]
            u_ref[0:SUBLANES, :] = conv0_ref[0]
            kh_ref[0:WINDOW, :] = _bf16(k0_ref[0])
            vh_ref[0:WINDOW, :] = _bf16(v0_ref[0])
        else:
            c_ref[...] = jnp.zeros_like(c_ref)
            n_ref[...] = jnp.zeros_like(n_ref)
            m_ref[...] = jnp.zeros_like(m_ref)
            u_ref[0:SUBLANES, :] = jnp.zeros((SUBLANES, QK_WIDTH), jnp.float32)
            kh_ref[0:WINDOW, :] = jnp.zeros((WINDOW, A_WIDTH), jnp.bfloat16)
            vh_ref[0:WINDOW, :] = jnp.zeros((WINDOW, A_WIDTH), jnp.bfloat16)

    x = x_ref[0]
    inv = lax.rsqrt(jnp.mean(x * x, axis=-1, keepdims=True) + EPS)
    hb_ref[...] = _bf16(x * inv * ng_ref[...])
    for j in range(0, QK_WIDTH, PROJ_BLOCK):
        u_ref[SUBLANES:SUBLANES + tile, j:j + PROJ_BLOCK] = (
            _dot(hb_ref[...], w_ref[:, j:j + PROJ_BLOCK]) + b_ref[:, j:j + PROJ_BLOCK])
    for j in range(0, Z_WIDTH, PROJ_BLOCK):
        wd = min(PROJ_BLOCK, Z_WIDTH - j)
        z_ref[:, j:j + wd] = (_dot(hb_ref[...], w_ref[:, QK_WIDTH + j:QK_WIDTH + j + wd])
                              + b_ref[:, QK_WIDTH + j:QK_WIDTH + j + wd])
    g_rows = lax.dot_general(wgt_ref[...], hb_ref[...], _NT,
                             preferred_element_type=jnp.float32) + bgt_ref[...]

    for j in range(0, QK_WIDTH, LANES):
        cs = slice(j, j + LANES)
        acc = cb_ref[:, cs] + cw_ref[CONV_WIDTH - 1:CONV_WIDTH, cs] * u_ref[SUBLANES:SUBLANES + tile, cs]
        for d in range(1, CONV_WIDTH):
            acc = acc + (cw_ref[CONV_WIDTH - 1 - d:CONV_WIDTH - d, cs]
                         * u_ref[SUBLANES - d:SUBLANES - d + tile, cs])
        act = _silu(acc)
        if j < M_WIDTH:
            qs_ref[:, cs] = act
        else:
            ks_ref[:, j - M_WIDTH:j - M_WIDTH + LANES] = act * (M_HEAD_DIM ** -0.5)

    @pl.when(t == n_tiles - 1)
    def _conv_out():
        conv_out_ref[0] = u_ref[tile:tile + SUBLANES, :]

    if n_tiles > 1:
        u_ref[0:SUBLANES, :] = u_ref[tile:tile + SUBLANES, :]

    row = lax.broadcasted_iota(jnp.int32, (1, chunk, chunk), 1)
    col = lax.broadcasted_iota(jnp.int32, (1, chunk, chunk), 2)
    lower = row >= col
    upper = row <= col
    neg_inf = jnp.float32(-jnp.inf)

    def chunk_rows(r):
        return jnp.stack([r[:, c * chunk:(c + 1) * chunk] for c in range(n_chunks)], axis=0)

    def chunked(a):
        return a.reshape(n_chunks, chunk, a.shape[-1])

    for h in range(M_HEADS):
        cs = slice(h * M_HEAD_DIM, (h + 1) * M_HEAD_DIM)
        q = chunked(qs_ref[:, cs])
        k = chunked(ks_ref[:, cs])
        qb, kb = _bf16(q), _bf16(k)
        vb = _bf16(chunked(z_ref[:, Z_MV + h * M_HEAD_DIM:Z_MV + (h + 1) * M_HEAD_DIM]))
        ig_row = chunk_rows(g_rows[h:h + 1, :])
        lf_row = _log_sigmoid(chunk_rows(g_rows[M_HEADS + h:M_HEADS + h + 1, :]))
        ig_col = chunked(z_ref[:, Z_GATE + h:Z_GATE + h + 1])
        lf_col = _log_sigmoid(chunked(z_ref[:, Z_GATE + M_HEADS + h:Z_GATE + M_HEADS + h + 1]))
        b_col = jnp.sum(jnp.where(lower, lf_row, 0.0), axis=2, keepdims=True)
        b_row = jnp.sum(jnp.where(upper, lf_col, 0.0), axis=1, keepdims=True)
        a_row = ig_row - b_row
        a_col = ig_col - b_col
        m_loc = jnp.max(jnp.where(lower, a_row, neg_inf), axis=2, keepdims=True)
        s = (jnp.einsum("ctd,csd->cts", qb, kb, preferred_element_type=jnp.float32)
             * jnp.where(lower, jnp.exp(a_row - m_loc), 0.0))
        pv = jnp.einsum("cts,csd->ctd", _bf16(s), vb, preferred_element_type=jnp.float32)
        s_sum = jnp.sum(s, axis=2, keepdims=True)
        m_loc_end = m_loc[:, chunk - 1:chunk, :]
        b_end = b_col[:, chunk - 1:chunk, :]
        kw = k * jnp.exp(a_col - m_loc_end)
        kwb = _bf16(kw)
        k_sum = jnp.sum(kw, axis=1, keepdims=True)
        c_run = c_ref[0, h]
        n_run = n_ref[h:h + 1, :]
        m_run = m_ref[h:h + 1, 0:1]
        c_start, n_start, m_start = [], [], []
        for c in range(n_chunks):
            c_start.append(_bf16(c_run))
            n_start.append(n_run)
            m_start.append(m_run)
            m_end = jnp.maximum(m_loc_end[c], m_run)
            w_new = jnp.exp(m_loc_end[c] - m_end)
            w_old = jnp.exp(m_run - m_end)
            kv = lax.dot_general(kwb[c], vb[c], _TN, preferred_element_type=jnp.float32)
            c_run = w_old * c_run + w_new * kv
            n_run = w_old * n_run + w_new * k_sum[c]
            m_run = b_end[c] + m_end
        c_ref[0, h] = c_run
        n_ref[h:h + 1, :] = n_run
        m_ref[h:h + 1, :] = jnp.broadcast_to(m_run, (1, LANES))
        m0 = jnp.stack(m_start, axis=0)
        m_max = jnp.maximum(m_loc, m0)
        w_intra = jnp.exp(m_loc - m_max)
        w_state = jnp.exp(m0 - m_max)
        qc = jnp.einsum("ctk,ckv->ctv", qb, jnp.stack(c_start, axis=0), preferred_element_type=jnp.float32)
        num = w_intra * pv + w_state * qc
        den = w_intra * s_sum + w_state * jnp.sum(q * jnp.stack(n_start, axis=0), axis=2, keepdims=True)
        hh = num / jnp.maximum(jnp.abs(den), jnp.exp(-(b_col + m_max)))
        hn = hh * lax.rsqrt(jnp.mean(hh * hh, axis=2, keepdims=True) + EPS) * mg_ref[h:h + 1, :]
        hn = hn.reshape(tile, M_HEAD_DIM)
        hm = _sigmoid(z_ref[:, Z_MO + h * M_HEAD_DIM:Z_MO + (h + 1) * M_HEAD_DIM]) * hn
        gmix_ref[:, cs] = _bf16(hm * _silu(z_ref[:, Z_MZ + h * M_HEAD_DIM:Z_MZ + (h + 1) * M_HEAD_DIM]))

    @pl.when(t == n_tiles - 1)
    def _state_out():
        n_out_ref[0] = n_ref[0:M_HEADS, :]
        m_out_ref[0] = m_ref[0:M_HEADS, :]

    lo = lax.broadcasted_iota(jnp.int32, (1, LANES), 1) < A_HEAD_DIM

    def head_norm(a, g):
        sq = a * a
        s_lo = jnp.sum(jnp.where(lo, sq, 0.0), axis=1, keepdims=True)
        s_hi = jnp.sum(jnp.where(lo, 0.0, sq), axis=1, keepdims=True)
        ms = jnp.where(lo, s_lo, s_hi) * (1.0 / A_HEAD_DIM)
        return a * lax.rsqrt(ms + EPS) * g

    if n_tiles > 1:
        ring = pl.multiple_of((t * tile) % WINDOW, tile)
    for p in range(A_PAIRS):
        cs = slice(p * LANES, (p + 1) * LANES)
        kn = head_norm(z_ref[:, Z_AK + p * LANES:Z_AK + (p + 1) * LANES], kg_ref[...])
        vv = z_ref[:, Z_AV + p * LANES:Z_AV + (p + 1) * LANES]
        kh_ref[WINDOW:WINDOW + tile, cs] = _bf16(kn)
        vh_ref[WINDOW:WINDOW + tile, cs] = _bf16(vv)
        if n_tiles > 1:
            k_out_ref[0, pl.ds(ring, tile), cs] = kn
            v_out_ref[0, pl.ds(ring, tile), cs] = vv
        else:
            k_out_ref[0, :, cs] = kn
            v_out_ref[0, :, cs] = vv

    kcol = lax.broadcasted_iota(jnp.int32, (1, band), 1)
    for c in range(n_chunks):
        rs = slice(c * chunk, (c + 1) * chunk)
        bs = slice(c * chunk, c * chunk + band)
        for p in range(A_PAIRS):
            cs = slice(p * LANES, (p + 1) * LANES)
            qn = head_norm(z_ref[rs, Z_AQ + p * LANES:Z_AQ + (p + 1) * LANES], qg_ref[...])
            qn = qn * (A_HEAD_DIM ** -0.5)
            q2 = _bf16(jnp.concatenate([jnp.where(lo, qn, 0.0), jnp.where(lo, 0.0, qn)], axis=0))
            s = lax.dot_general(q2, kh_ref[bs, cs], _NT, preferred_element_type=jnp.float32) + bias_ref[p]
            if not has_state:
                first_valid = WINDOW - t * tile - c * chunk
                s = jnp.where(kcol >= first_valid, s, neg_inf)
            e = jnp.exp(s - jnp.max(s, axis=1, keepdims=True))
            o2 = _dot(_bf16(e), vh_ref[bs, cs]) / jnp.sum(e, axis=1, keepdims=True)
            o = jnp.where(lo, o2[0:chunk], o2[chunk:2 * chunk])
            gatt_ref[rs, cs] = _bf16(o * _silu(z_ref[rs, Z_AZ + p * LANES:Z_AZ + (p + 1) * LANES]))

    if n_tiles > 1:
        for r in range(0, WINDOW, LANES):
            kh_ref[r:r + LANES, :] = kh_ref[tile + r:tile + r + LANES, :]
            vh_ref[r:r + LANES, :] = vh_ref[tile + r:tile + r + LANES, :]

    for j in range(0, D_MODEL, PROJ_BLOCK):
        cs = slice(j, j + PROJ_BLOCK)
        u_m = _dot(gmix_ref[...], wbm_ref[:, cs])
        u_a = _dot(gatt_ref[...], wba_ref[:, cs])
        mix_ref[:, cs] = _bf16(_sigmoid(z_ref[:, Z_GM + j:Z_GM + j + PROJ_BLOCK]) * u_m
                               + _sigmoid(z_ref[:, Z_GA + j:Z_GA + j + PROJ_BLOCK]) * u_a)
    for j in range(0, D_MODEL, PROJ_BLOCK):
        cs = slice(j, j + PROJ_BLOCK)
        y_ref[0, :, cs] = x_ref[0, :, cs] + _dot(mix_ref[...], wout_ref[:, cs])


def _const_spec(shape):
    zeros = (0,) * len(shape)
    return pl.BlockSpec(shape, lambda b, t: zeros, pipeline_mode=pl.Buffered(1))


def _layer_call(x, state, params, *, tile, chunk):
    n_streams, seq, _ = x.shape
    n_tiles = seq // tile
    has_state = state is not None
    keep = min(WINDOW, seq)
    band = WINDOW + chunk

    def per_stream(shape):
        nd = len(shape)
        return pl.BlockSpec((1,) + tuple(shape[1:]), lambda b, t: (b,) + (0,) * (nd - 1))

    in_specs = [pl.BlockSpec((1, tile, D_MODEL), lambda b, t: (b, t, 0))]
    args = [x]
    if has_state:
        for a in state:
            in_specs.append(per_stream(a.shape))
            args.append(a)
    for a in params:
        in_specs.append(_const_spec(a.shape))
        args.append(a)

    f32 = jnp.float32
    out_shape = (
        jax.ShapeDtypeStruct((n_streams, seq, D_MODEL), f32),
        jax.ShapeDtypeStruct((n_streams, M_HEADS, M_HEAD_DIM, M_HEAD_DIM), f32),
        jax.ShapeDtypeStruct((n_streams, M_HEADS, LANES), f32),
        jax.ShapeDtypeStruct((n_streams, M_HEADS, LANES), f32),
        jax.ShapeDtypeStruct((n_streams, SUBLANES, QK_WIDTH), f32),
        jax.ShapeDtypeStruct((n_streams, keep, A_WIDTH), f32),
        jax.ShapeDtypeStruct((n_streams, keep, A_WIDTH), f32),
    )
    out_specs = (pl.BlockSpec((1, tile, D_MODEL), lambda b, t: (b, t, 0)),) + tuple(
        per_stream(s.shape) for s in out_shape[1:])

    bf16 = jnp.bfloat16
    scratch = [
        pltpu.VMEM((tile, D_MODEL), bf16),
        pltpu.VMEM((tile + SUBLANES, QK_WIDTH), f32),
        pltpu.VMEM((tile, Z_WIDTH), f32),
        pltpu.VMEM((tile, M_WIDTH), f32),
        pltpu.VMEM((tile, M_WIDTH), f32),
        pltpu.VMEM((WINDOW + tile, A_WIDTH), bf16),
        pltpu.VMEM((WINDOW + tile, A_WIDTH), bf16),
        pltpu.VMEM((tile, M_WIDTH), bf16),
        pltpu.VMEM((tile, A_WIDTH), bf16),
        pltpu.VMEM((tile, D_MODEL), bf16),
        pltpu.VMEM((SUBLANES, LANES), f32),
        pltpu.VMEM((SUBLANES, LANES), f32),
    ]
    kern = functools.partial(_layer_kernel, tile=tile, chunk=chunk, n_tiles=n_tiles, has_state=has_state)
    return pl.pallas_call(
        kern,
        grid=(n_streams, n_tiles),
        in_specs=in_specs,
        out_specs=out_specs,
        out_shape=out_shape,
        scratch_shapes=scratch,
        compiler_params=pltpu.CompilerParams(
            dimension_semantics=("arbitrary", "arbitrary"),
            vmem_limit_bytes=VMEM_LIMIT_BYTES),
        name="layer_state" if has_state else "layer_fresh",
    )(*args)


def _pair_bias(rel_bias, chunk):
    assert chunk - 1 <= REL_CLIP <= WINDOW
    band = WINDOW + chunk
    n_far = chunk - 1 + WINDOW - REL_CLIP
    n_ext = band + chunk - 1
    rev = rel_bias[:, ::-1].astype(jnp.float32)
    ext = jnp.concatenate([jnp.broadcast_to(rev[:, :1], (A_HEADS, n_far)), rev[:, :n_ext - n_far]], axis=1)
    bias = jnp.stack([ext[:, chunk - 1 - t:chunk - 1 - t + band] for t in range(chunk)], axis=1)
    return bias.reshape(A_PAIRS, 2 * chunk, band)


def _layer_params(norm_g, w_in, b_in, conv_w, conv_b, m_head_g, q_norm_g, k_norm_g, rel_bias,
                  w_bm, w_ba, w_out, chunk):
    points = [int(p) for p in np.cumsum(IN_WIDTHS)[:-1]]
    wq, wk, wv, wo, wz, wi, wf, waq, wak, wav, waz, wgm, wga = jnp.split(w_in, points, axis=1)
    bq, bk, bv, bo, bz, bi, bf, baq, bak, bav, baz, bgm, bga = jnp.split(b_in, points)
    w_gate = jnp.concatenate([wi, wf], axis=1)
    b_gate = jnp.concatenate([bi, bf])
    pad = LANES - 2 * M_HEADS
    w_main = jnp.concatenate([wq, wk, wv, wo, wz, waq, wak, wav, waz, wgm, wga,
                              jnp.pad(w_gate, ((0, 0), (0, pad)))], axis=1).astype(jnp.bfloat16)
    b_main = jnp.concatenate([bq, bk, bv, bo, bz, baq, bak, bav, baz, bgm, bga,
                              jnp.pad(b_gate, (0, pad))])[None, :]
    return (norm_g[None, :], w_main, b_main, w_gate.T.astype(jnp.bfloat16), b_gate[:, None],
            conv_w, conv_b[None, :], m_head_g, jnp.tile(q_norm_g, 2)[None, :], jnp.tile(k_norm_g, 2)[None, :],
            _pair_bias(rel_bias, chunk), w_bm.astype(jnp.bfloat16), w_ba.astype(jnp.bfloat16),
            w_out.astype(jnp.bfloat16))


PROMPT_TILE = 256


def kernel(x_prompt, x_sample, state_mlstm_C, state_mlstm_n, state_mlstm_m, state_mlstm_conv,
           cache_attn_k, cache_attn_v, norm_g, w_in, b_in, conv_w, conv_b, m_head_g,
           q_norm_g, k_norm_g, rel_bias, w_bm, w_ba, w_out):
    depth = w_in.shape[0]
    xp, xs = x_prompt, x_sample
    n_p, t_p, _ = xp.shape
    n_s, t_s, _ = xs.shape
    outs_p, outs_s = [], []
    for l in range(depth):
        weights = (norm_g[l], w_in[l], b_in[l], conv_w[l], conv_b[l], m_head_g[l], q_norm_g[l],
                   k_norm_g[l], rel_bias[l], w_bm[l], w_ba[l], w_out[l])
        xp, c_p, n_p_, m_p, conv_p, k_p, v_p = _layer_call(
            xp, None, _layer_params(*weights, chunk=CHUNK), tile=PROMPT_TILE, chunk=CHUNK)
        outs_p.append((c_p, n_p_, m_p[:, :, 0], conv_p[:, SUBLANES - (CONV_WIDTH - 1):],
                       k_p.reshape(n_p, -1, A_HEADS, A_HEAD_DIM), v_p.reshape(n_p, -1, A_HEADS, A_HEAD_DIM)))
        state = (state_mlstm_C[l], state_mlstm_n[l],
                 jnp.broadcast_to(state_mlstm_m[l][:, :, None], (n_s, M_HEADS, LANES)),
                 jnp.pad(state_mlstm_conv[l], ((0, 0), (SUBLANES - (CONV_WIDTH - 1), 0), (0, 0))),
                 cache_attn_k[l].reshape(n_s, -1, A_WIDTH), cache_attn_v[l].reshape(n_s, -1, A_WIDTH))
        xs, c_s, n_s_, m_s, conv_s, k_s, v_s = _layer_call(
            xs, state, _layer_params(*weights, chunk=t_s), tile=t_s, chunk=t_s)
        outs_s.append((c_s, n_s_, m_s[:, :, 0], conv_s[:, SUBLANES - (CONV_WIDTH - 1):],
                       k_s.reshape(n_s, -1, A_HEADS, A_HEAD_DIM), v_s.reshape(n_s, -1, A_HEADS, A_HEAD_DIM)))
    stack = lambda outs, i: jnp.stack([o[i] for o in outs])
    return (xp, xs) + tuple(stack(outs_p, i) for i in range(6)) + tuple(stack(outs_s, i) for i in range(6))
```

```python
import functools

import jax
import jax.numpy as jnp
import numpy as np
from jax import lax
from jax.experimental import pallas as pl
from jax.experimental.pallas import tpu as pltpu

D_MODEL = 1024
CHUNK = 64
M_HEADS = 4
M_HEAD_DIM = 128
M_WIDTH = M_HEADS * M_HEAD_DIM
CONV_WIDTH = 4
A_HEADS = 8
A_HEAD_DIM = 64
A_WIDTH = A_HEADS * A_HEAD_DIM
A_PAIRS = A_HEADS // 2
WINDOW = 8 * CHUNK
REL_CLIP = 128
EPS = 1e-6
IN_WIDTHS = (M_WIDTH, M_WIDTH, M_WIDTH, M_WIDTH, M_WIDTH, M_HEADS, M_HEADS,
             A_WIDTH, A_WIDTH, A_WIDTH, A_WIDTH, D_MODEL, D_MODEL)

LANES = 128
SUBLANES = 8
VMEM_LIMIT_BYTES = 56 * 1024 * 1024

Z_MV = 0
Z_MO = Z_MV + M_WIDTH
Z_MZ = Z_MO + M_WIDTH
Z_AQ = Z_MZ + M_WIDTH
Z_AK = Z_AQ + A_WIDTH
Z_AV = Z_AK + A_WIDTH
Z_AZ = Z_AV + A_WIDTH
Z_GM = Z_AZ + A_WIDTH
Z_GA = Z_GM + D_MODEL
Z_GATE = Z_GA + D_MODEL
Z_WIDTH = Z_GATE + LANES
QK_WIDTH = 2 * M_WIDTH
PROJ_BLOCK = 512

_NT = (((1,), (1,)), ((), ()))
_TN = (((0,), (0,)), ((), ()))


def _sigmoid(x):
    return 1.0 / (1.0 + jnp.exp(-x))


def _silu(x):
    return x * _sigmoid(x)


def _log_sigmoid(x):
    return jnp.minimum(x, 0.0) - jnp.log1p(jnp.exp(-jnp.abs(x)))


def _bf16(x):
    return x.astype(jnp.bfloat16)


def _dot(a, b):
    return jnp.dot(a, b, preferred_element_type=jnp.float32)


def _layer_kernel(*refs, tile, chunk, n_tiles, has_state):
    n_chunks = tile // chunk
    band = WINDOW + chunk
    it = iter(refs)
    x_ref = next(it)
    if has_state:
        c0_ref, n0_ref, m0_ref, conv0_ref, k0_ref, v0_ref = (next(it) for _ in range(6))
    (ng_ref, w_ref, b_ref, wgt_ref, bgt_ref, cw_ref, cb_ref, mg_ref, qg_ref, kg_ref,
     bias_ref, wbm_ref, wba_ref, wout_ref) = (next(it) for _ in range(14))
    y_ref, c_ref, n_out_ref, m_out_ref, conv_out_ref, k_out_ref, v_out_ref = (next(it) for _ in range(7))
    (hb_ref, u_ref, z_ref, qs_ref, ks_ref, kh_ref, vh_ref, gmix_ref, gatt_ref, mix_ref,
     n_ref, m_ref) = (next(it) for _ in range(12))

    t = pl.program_id(1)

    @pl.when(t == 0)
    def _init():
        if has_state:
            c_ref[0] = c0_ref[0]
            n_ref[0:M_HEADS, :] = n0_ref[0]
            m_ref[0:M_HEADS, :] = m0_ref[0]
            u_ref[0:SUBLANES, :] = conv0_ref[0]
            kh_ref[0:WINDOW, :] = _bf16(k0_ref[0])
            vh_ref[0:WINDOW, :] = _bf16(v0_ref[0])
        else:
            c_ref[...] = jnp.zeros_like(c_ref)
            n_ref[...] = jnp.zeros_like(n_ref)
            m_ref[...] = jnp.zeros_like(m_ref)
            u_ref[0:SUBLANES, :] = jnp.zeros((SUBLANES, QK_WIDTH), jnp.float32)
            kh_ref[0:WINDOW, :] = jnp.zeros((WINDOW, A_WIDTH), jnp.bfloat16)
            vh_ref[0:WINDOW, :] = jnp.zeros((WINDOW, A_WIDTH), jnp.bfloat16)

    x = x_ref[0]
    inv = lax.rsqrt(jnp.mean(x * x, axis=-1, keepdims=True) + EPS)
    hb_ref[...] = _bf16(x * inv * ng_ref[...])
    for j in range(0, QK_WIDTH, PROJ_BLOCK):
        u_ref[SUBLANES:SUBLANES + tile, j:j + PROJ_BLOCK] = (
            _dot(hb_ref[...], w_ref[:, j:j + PROJ_BLOCK]) + b_ref[:, j:j + PROJ_BLOCK])

    def project(j):
        wd = min(PROJ_BLOCK, Z_WIDTH - j)
        z_ref[:, j:j + wd] = (_dot(hb_ref[...], w_ref[:, QK_WIDTH + j:QK_WIDTH + j + wd])
                              + b_ref[:, QK_WIDTH + j:QK_WIDTH + j + wd])

    for j in (Z_MV, Z_MO, Z_MZ, Z_GATE):
        project(j)
    g_rows = lax.dot_general(wgt_ref[...], hb_ref[...], _NT,
                             preferred_element_type=jnp.float32) + bgt_ref[...]

    for j in range(0, QK_WIDTH, LANES):
        cs = slice(j, j + LANES)
        acc = cb_ref[:, cs] + cw_ref[CONV_WIDTH - 1:CONV_WIDTH, cs] * u_ref[SUBLANES:SUBLANES + tile, cs]
        for d in range(1, CONV_WIDTH):
            acc = acc + (cw_ref[CONV_WIDTH - 1 - d:CONV_WIDTH - d, cs]
                         * u_ref[SUBLANES - d:SUBLANES - d + tile, cs])
        act = _silu(acc)
        if j < M_WIDTH:
            qs_ref[:, cs] = act
        else:
            ks_ref[:, j - M_WIDTH:j - M_WIDTH + LANES] = act * (M_HEAD_DIM ** -0.5)

    @pl.when(t == n_tiles - 1)
    def _conv_out():
        conv_out_ref[0] = u_ref[tile:tile + SUBLANES, :]

    if n_tiles > 1:
        u_ref[0:SUBLANES, :] = u_ref[tile:tile + SUBLANES, :]

    row = lax.broadcasted_iota(jnp.int32, (1, chunk, chunk), 1)
    col = lax.broadcasted_iota(jnp.int32, (1, chunk, chunk), 2)
    lower = row >= col
    upper = row <= col
    neg_inf = jnp.float32(-jnp.inf)

    def chunk_rows(r):
        return jnp.stack([r[:, c * chunk:(c + 1) * chunk] for c in range(n_chunks)], axis=0)

    def chunked(a):
        return a.reshape(n_chunks, chunk, a.shape[-1])

    for h in range(M_HEADS):
        cs = slice(h * M_HEAD_DIM, (h + 1) * M_HEAD_DIM)
        q = chunked(qs_ref[:, cs])
        k = chunked(ks_ref[:, cs])
        qb, kb = _bf16(q), _bf16(k)
        vb = _bf16(chunked(z_ref[:, Z_MV + h * M_HEAD_DIM:Z_MV + (h + 1) * M_HEAD_DIM]))
        ig_row = chunk_rows(g_rows[h:h + 1, :])
        lf_row = _log_sigmoid(chunk_rows(g_rows[M_HEADS + h:M_HEADS + h + 1, :]))
        ig_col = chunked(z_ref[:, Z_GATE + h:Z_GATE + h + 1])
        lf_col = _log_sigmoid(chunked(z_ref[:, Z_GATE + M_HEADS + h:Z_GATE + M_HEADS + h + 1]))
        b_col = jnp.sum(jnp.where(lower, lf_row, 0.0), axis=2, keepdims=True)
        b_row = jnp.sum(jnp.where(upper, lf_col, 0.0), axis=1, keepdims=True)
        a_row = ig_row - b_row
        a_col = ig_col - b_col
        m_loc = jnp.max(jnp.where(lower, a_row, neg_inf), axis=2, keepdims=True)
        s = (jnp.einsum("ctd,csd->cts", qb, kb, preferred_element_type=jnp.float32)
             * jnp.where(lower, jnp.exp(a_row - m_loc), 0.0))
        pv = jnp.einsum("cts,csd->ctd", _bf16(s), vb, preferred_element_type=jnp.float32)
        s_sum = jnp.sum(s, axis=2, keepdims=True)
        m_loc_end = m_loc[:, chunk - 1:chunk, :]
        b_end = b_col[:, chunk - 1:chunk, :]
        kw = k * jnp.exp(a_col - m_loc_end)
        kwb = _bf16(kw)
        k_sum = jnp.sum(kw, axis=1, keepdims=True)
        c_run = c_ref[0, h]
        n_run = n_ref[h:h + 1, :]
        m_run = m_ref[h:h + 1, 0:1]
        c_start, n_start, m_start = [], [], []
        for c in range(n_chunks):
            c_start.append(_bf16(c_run))
            n_start.append(n_run)
            m_start.append(m_run)
            m_end = jnp.maximum(m_loc_end[c], m_run)
            w_new = jnp.exp(m_loc_end[c] - m_end)
            w_old = jnp.exp(m_run - m_end)
            kv = lax.dot_general(kwb[c], vb[c], _TN, preferred_element_type=jnp.float32)
            c_run = w_old * c_run + w_new * kv
            n_run = w_old * n_run + w_new * k_sum[c]
            m_run = b_end[c] + m_end
        c_ref[0, h] = c_run
        n_ref[h:h + 1, :] = n_run
        m_ref[h:h + 1, :] = jnp.broadcast_to(m_run, (1, LANES))
        m0 = jnp.stack(m_start, axis=0)
        m_max = jnp.maximum(m_loc, m0)
        w_intra = jnp.exp(m_loc - m_max)
        w_state = jnp.exp(m0 - m_max)
        qc = jnp.einsum("ctk,ckv->ctv", qb, jnp.stack(c_start, axis=0), preferred_element_type=jnp.float32)
        num = w_intra * pv + w_state * qc
        den = w_intra * s_sum + w_state * jnp.sum(q * jnp.stack(n_start, axis=0), axis=2, keepdims=True)
        hh = num / jnp.maximum(jnp.abs(den), jnp.exp(-(b_col + m_max)))
        hn = hh * lax.rsqrt(jnp.mean(hh * hh, axis=2, keepdims=True) + EPS) * mg_ref[h:h + 1, :]
        hn = hn.reshape(tile, M_HEAD_DIM)
        hm = _sigmoid(z_ref[:, Z_MO + h * M_HEAD_DIM:Z_MO + (h + 1) * M_HEAD_DIM]) * hn
        gmix_ref[:, cs] = _bf16(hm * _silu(z_ref[:, Z_MZ + h * M_HEAD_DIM:Z_MZ + (h + 1) * M_HEAD_DIM]))
        project(Z_AQ + h * PROJ_BLOCK)

    @pl.when(t == n_tiles - 1)
    def _state_out():
        n_out_ref[0] = n_ref[0:M_HEADS, :]
        m_out_ref[0] = m_ref[0:M_HEADS, :]

    lo = lax.broadcasted_iota(jnp.int32, (1, LANES), 1) < A_HEAD_DIM

    def head_norm(a, g):
        sq = a * a
        s_lo = jnp.sum(jnp.where(lo, sq, 0.0), axis=1, keepdims=True)
        s_hi = jnp.sum(jnp.where(lo, 0.0, sq), axis=1, keepdims=True)
        ms = jnp.where(lo, s_lo, s_hi) * (1.0 / A_HEAD_DIM)
        return a * lax.rsqrt(ms + EPS) * g

    if n_tiles > 1:
        ring = pl.multiple_of((t * tile) % WINDOW, tile)
    for p in range(A_PAIRS):
        cs = slice(p * LANES, (p + 1) * LANES)
        kn = head_norm(z_ref[:, Z_AK + p * LANES:Z_AK + (p + 1) * LANES], kg_ref[...])
        vv = z_ref[:, Z_AV + p * LANES:Z_AV + (p + 1) * LANES]
        kh_ref[WINDOW:WINDOW + tile, cs] = _bf16(kn)
        vh_ref[WINDOW:WINDOW + tile, cs] = _bf16(vv)
        if n_tiles > 1:
            k_out_ref[0, pl.ds(ring, tile), cs] = kn
            v_out_ref[0, pl.ds(ring, tile), cs] = vv
        else:
            k_out_ref[0, :, cs] = kn
            v_out_ref[0, :, cs] = vv

    if not has_state:
        kcol = lax.broadcasted_iota(jnp.int32, (n_chunks, 1, band), 2)
        first_valid = WINDOW - t * tile - chunk * lax.broadcasted_iota(jnp.int32, (n_chunks, 1, band), 0)
        visible = kcol >= first_valid
    bands = [slice(c * chunk, c * chunk + band) for c in range(n_chunks)]
    for p in range(A_PAIRS):
        cs = slice(p * LANES, (p + 1) * LANES)
        qn = head_norm(z_ref[:, Z_AQ + p * LANES:Z_AQ + (p + 1) * LANES], qg_ref[...])
        qn = chunked(qn * (A_HEAD_DIM ** -0.5))
        q2 = _bf16(jnp.concatenate([jnp.where(lo, qn, 0.0), jnp.where(lo, 0.0, qn)], axis=1))
        s = jnp.stack([lax.dot_general(q2[c], kh_ref[bands[c], cs], _NT, preferred_element_type=jnp.float32)
                       for c in range(n_chunks)], axis=0) + bias_ref[p]
        if not has_state:
            s = jnp.where(visible, s, neg_inf)
        e = jnp.exp(s - jnp.max(s, axis=2, keepdims=True))
        eb = _bf16(e)
        o2 = jnp.stack([_dot(eb[c], vh_ref[bands[c], cs]) for c in range(n_chunks)], axis=0)
        o2 = o2 / jnp.sum(e, axis=2, keepdims=True)
        o = jnp.where(lo, o2[:, 0:chunk], o2[:, chunk:2 * chunk]).reshape(tile, LANES)
        gatt_ref[:, cs] = _bf16(o * _silu(z_ref[:, Z_AZ + p * LANES:Z_AZ + (p + 1) * LANES]))
        project(Z_GM + p * PROJ_BLOCK)

    if n_tiles > 1:
        for r in range(0, WINDOW, LANES):
            kh_ref[r:r + LANES, :] = kh_ref[tile + r:tile + r + LANES, :]
            vh_ref[r:r + LANES, :] = vh_ref[tile + r:tile + r + LANES, :]

    for j in range(0, D_MODEL, PROJ_BLOCK):
        cs = slice(j, j + PROJ_BLOCK)
        u_m = _dot(gmix_ref[...], wbm_ref[:, cs])
        u_a = _dot(gatt_ref[...], wba_ref[:, cs])
        mix_ref[:, cs] = _bf16(_sigmoid(z_ref[:, Z_GM + j:Z_GM + j + PROJ_BLOCK]) * u_m
                               + _sigmoid(z_ref[:, Z_GA + j:Z_GA + j + PROJ_BLOCK]) * u_a)
    for j in range(0, D_MODEL, PROJ_BLOCK):
        cs = slice(j, j + PROJ_BLOCK)
        y_ref[0, :, cs] = x_ref[0, :, cs] + _dot(mix_ref[...], wout_ref[:, cs])


def _const_spec(shape):
    zeros = (0,) * len(shape)
    return pl.BlockSpec(shape, lambda b, t: zeros, pipeline_mode=pl.Buffered(1))


def _layer_call(x, state, params, *, tile, chunk):
    n_streams, seq, _ = x.shape
    n_tiles = seq // tile
    has_state = state is not None
    keep = min(WINDOW, seq)
    band = WINDOW + chunk

    def per_stream(shape):
        nd = len(shape)
        return pl.BlockSpec((1,) + tuple(shape[1:]), lambda b, t: (b,) + (0,) * (nd - 1))

    in_specs = [pl.BlockSpec((1, tile, D_MODEL), lambda b, t: (b, t, 0))]
    args = [x]
    if has_state:
        for a in state:
            in_specs.append(per_stream(a.shape))
            args.append(a)
    for a in params:
        in_specs.append(_const_spec(a.shape))
        args.append(a)

    f32 = jnp.float32
    out_shape = (
        jax.ShapeDtypeStruct((n_streams, seq, D_MODEL), f32),
        jax.ShapeDtypeStruct((n_streams, M_HEADS, M_HEAD_DIM, M_HEAD_DIM), f32),
        jax.ShapeDtypeStruct((n_streams, M_HEADS, LANES), f32),
        jax.ShapeDtypeStruct((n_streams, M_HEADS, LANES), f32),
        jax.ShapeDtypeStruct((n_streams, SUBLANES, QK_WIDTH), f32),
        jax.ShapeDtypeStruct((n_streams, keep, A_WIDTH), f32),
        jax.ShapeDtypeStruct((n_streams, keep, A_WIDTH), f32),
    )
    out_specs = (pl.BlockSpec((1, tile, D_MODEL), lambda b, t: (b, t, 0)),) + tuple(
        per_stream(s.shape) for s in out_shape[1:])

    bf16 = jnp.bfloat16
    scratch = [
        pltpu.VMEM((tile, D_MODEL), bf16),
        pltpu.VMEM((tile + SUBLANES, QK_WIDTH), f32),
        pltpu.VMEM((tile, Z_WIDTH), f32),
        pltpu.VMEM((tile, M_WIDTH), f32),
        pltpu.VMEM((tile, M_WIDTH), f32),
        pltpu.VMEM((WINDOW + tile, A_WIDTH), bf16),
        pltpu.VMEM((WINDOW + tile, A_WIDTH), bf16),
        pltpu.VMEM((tile, M_WIDTH), bf16),
        pltpu.VMEM((tile, A_WIDTH), bf16),
        pltpu.VMEM((tile, D_MODEL), bf16),
        pltpu.VMEM((SUBLANES, LANES), f32),
        pltpu.VMEM((SUBLANES, LANES), f32),
    ]
    kern = functools.partial(_layer_kernel, tile=tile, chunk=chunk, n_tiles=n_tiles, has_state=has_state)
    return pl.pallas_call(
        kern,
        grid=(n_streams, n_tiles),
        in_specs=in_specs,
        out_specs=out_specs,
        out_shape=out_shape,
        scratch_shapes=scratch,
        compiler_params=pltpu.CompilerParams(
            dimension_semantics=("arbitrary", "arbitrary"),
            vmem_limit_bytes=VMEM_LIMIT_BYTES),
        name="layer_state" if has_state else "layer_fresh",
    )(*args)


def _pair_bias(rel_bias, chunk):
    assert chunk - 1 <= REL_CLIP <= WINDOW
    band = WINDOW + chunk
    n_far = chunk - 1 + WINDOW - REL_CLIP
    n_ext = band + chunk - 1
    rev = rel_bias[:, ::-1].astype(jnp.float32)
    ext = jnp.concatenate([jnp.broadcast_to(rev[:, :1], (A_HEADS, n_far)), rev[:, :n_ext - n_far]], axis=1)
    bias = jnp.stack([ext[:, chunk - 1 - t:chunk - 1 - t + band] for t in range(chunk)], axis=1)
    return bias.reshape(A_PAIRS, 2 * chunk, band)


def _layer_params(norm_g, w_in, b_in, conv_w, conv_b, m_head_g, q_norm_g, k_norm_g, rel_bias,
                  w_bm, w_ba, w_out, chunk):
    points = [int(p) for p in np.cumsum(IN_WIDTHS)[:-1]]
    wq, wk, wv, wo, wz, wi, wf, waq, wak, wav, waz, wgm, wga = jnp.split(w_in, points, axis=1)
    bq, bk, bv, bo, bz, bi, bf, baq, bak, bav, baz, bgm, bga = jnp.split(b_in, points)
    w_gate = jnp.concatenate([wi, wf], axis=1)
    b_gate = jnp.concatenate([bi, bf])
    pad = LANES - 2 * M_HEADS
    w_main = jnp.concatenate([wq, wk, wv, wo, wz, waq, wak, wav, waz, wgm, wga,
                              jnp.pad(w_gate, ((0, 0), (0, pad)))], axis=1).astype(jnp.bfloat16)
    b_main = jnp.concatenate([bq, bk, bv, bo, bz, baq, bak, bav, baz, bgm, bga,
                              jnp.pad(b_gate, (0, pad))])[None, :]
    return (norm_g[None, :], w_main, b_main, w_gate.T.astype(jnp.bfloat16), b_gate[:, None],
            conv_w, conv_b[None, :], m_head_g, jnp.tile(q_norm_g, 2)[None, :], jnp.tile(k_norm_g, 2)[None, :],
            _pair_bias(rel_bias, chunk), w_bm.astype(jnp.bfloat16), w_ba.astype(jnp.bfloat16),
            w_out.astype(jnp.bfloat16))


PROMPT_TILE = 256


def kernel(x_prompt, x_sample, state_mlstm_C, state_mlstm_n, state_mlstm_m, state_mlstm_conv,
           cache_attn_k, cache_attn_v, norm_g, w_in, b_in, conv_w, conv_b, m_head_g,
           q_norm_g, k_norm_g, rel_bias, w_bm, w_ba, w_out):
    depth = w_in.shape[0]
    xp, xs = x_prompt, x_sample
    n_p, t_p, _ = xp.shape
    n_s, t_s, _ = xs.shape
    outs_p, outs_s = [], []
    for l in range(depth):
        weights = (norm_g[l], w_in[l], b_in[l], conv_w[l], conv_b[l], m_head_g[l], q_norm_g[l],
                   k_norm_g[l], rel_bias[l], w_bm[l], w_ba[l], w_out[l])
        xp, c_p, n_p_, m_p, conv_p, k_p, v_p = _layer_call(
            xp, None, _layer_params(*weights, chunk=CHUNK), tile=PROMPT_TILE, chunk=CHUNK)
        outs_p.append((c_p, n_p_, m_p[:, :, 0], conv_p[:, SUBLANES - (CONV_WIDTH - 1):],
                       k_p.reshape(n_p, -1, A_HEADS, A_HEAD_DIM), v_p.reshape(n_p, -1, A_HEADS, A_HEAD_DIM)))
        state = (state_mlstm_C[l], state_mlstm_n[l],
                 jnp.broadcast_to(state_mlstm_m[l][:, :, None], (n_s, M_HEADS, LANES)),
                 jnp.pad(state_mlstm_conv[l], ((0, 0), (SUBLANES - (CONV_WIDTH - 1), 0), (0, 0))),
                 cache_attn_k[l].reshape(n_s, -1, A_WIDTH), cache_attn_v[l].reshape(n_s, -1, A_WIDTH))
        xs, c_s, n_s_, m_s, conv_s, k_s, v_s = _layer_call(
            xs, state, _layer_params(*weights, chunk=t_s), tile=t_s, chunk=t_s)
        outs_s.append((c_s, n_s_, m_s[:, :, 0], conv_s[:, SUBLANES - (CONV_WIDTH - 1):],
                       k_s.reshape(n_s, -1, A_HEADS, A_HEAD_DIM), v_s.reshape(n_s, -1, A_HEADS, A_HEAD_DIM)))
    stack = lambda outs, i: jnp.stack([o[i] for o in outs])
    return (xp, xs) + tuple(stack(outs_p, i) for i in range(6)) + tuple(stack(outs_s, i) for i in range(6))
```

```python
import functools

import jax
import jax.numpy as jnp
import numpy as np
from jax import lax
from jax.experimental import pallas as pl
from jax.experimental.pallas import tpu as pltpu

D_MODEL = 1024
CHUNK = 64
M_HEADS = 4
M_HEAD_DIM = 128
M_WIDTH = M_HEADS * M_HEAD_DIM
CONV_WIDTH = 4
A_HEADS = 8
A_HEAD_DIM = 64
A_WIDTH = A_HEADS * A_HEAD_DIM
A_PAIRS = A_HEADS // 2
WINDOW = 8 * CHUNK
REL_CLIP = 128
EPS = 1e-6
IN_WIDTHS = (M_WIDTH, M_WIDTH, M_WIDTH, M_WIDTH, M_WIDTH, M_HEADS, M_HEADS,
             A_WIDTH, A_WIDTH, A_WIDTH, A_WIDTH, D_MODEL, D_MODEL)

LANES = 128
SUBLANES = 8
VMEM_LIMIT_BYTES = 56 * 1024 * 1024

Z_MV = 0
Z_MO = Z_MV + M_WIDTH
Z_MZ = Z_MO + M_WIDTH
Z_AQ = Z_MZ + M_WIDTH
Z_AK = Z_AQ + A_WIDTH
Z_AV = Z_AK + A_WIDTH
Z_AZ = Z_AV + A_WIDTH
Z_GM = Z_AZ + A_WIDTH
Z_GA = Z_GM + D_MODEL
Z_WIDTH = Z_GA + D_MODEL
QK_WIDTH = 2 * M_WIDTH
PROJ_BLOCK = 512

_NT = (((1,), (1,)), ((), ()))


def _sigmoid(x):
    return 1.0 / (1.0 + jnp.exp(-x))


def _silu(x):
    return x * _sigmoid(x)


def _log_sigmoid(x):
    return jnp.minimum(x, 0.0) - jnp.log1p(jnp.exp(-jnp.abs(x)))


def _bf16(x):
    return x.astype(jnp.bfloat16)


def _dot(a, b):
    return jnp.dot(a, b, preferred_element_type=jnp.float32)


def _split3(x):
    hi = _bf16(x)
    r = x - hi.astype(jnp.float32)
    mid = _bf16(r)
    lo = _bf16(r - mid.astype(jnp.float32))
    return hi, mid, lo


def _layer_kernel(*refs, tile, chunk, n_tiles, has_state):
    n_chunks = tile // chunk
    band = WINDOW + chunk
    it = iter(refs)
    x_ref = next(it)
    if has_state:
        c0_ref, n0_ref, m0_ref, conv0_ref, k0_ref, v0_ref = (next(it) for _ in range(6))
    (ng_ref, w_ref, b_ref, wgt_ref, bgt_ref, cw_ref, cb_ref, mg_ref, qg_ref, kg_ref,
     bias_ref, wbm_ref, wba_ref, wout_ref) = (next(it) for _ in range(14))
    y_ref, c_ref, n_out_ref, m_out_ref, conv_out_ref, k_out_ref, v_out_ref = (next(it) for _ in range(7))
    (hb_ref, u_ref, z_ref, qs_ref, ks_ref, kh_ref, vh_ref, gmix_ref, gatt_ref, mix_ref,
     nd_ref, m_ref) = (next(it) for _ in range(12))

    t = pl.program_id(1)

    @pl.when(t == 0)
    def _init():
        if has_state:
            c_ref[0] = c0_ref[0]
            for h in range(M_HEADS):
                nd_ref[h] = jnp.broadcast_to(n0_ref[0, h:h + 1, :], (M_HEAD_DIM, M_HEAD_DIM)).T
            m_ref[0:M_HEADS, :] = m0_ref[0]
            u_ref[0:SUBLANES, :] = conv0_ref[0]
            kh_ref[0:WINDOW, :] = _bf16(k0_ref[0])
            vh_ref[0:WINDOW, :] = _bf16(v0_ref[0])
        else:
            c_ref[...] = jnp.zeros_like(c_ref)
            nd_ref[...] = jnp.zeros_like(nd_ref)
            m_ref[...] = jnp.zeros_like(m_ref)
            u_ref[0:SUBLANES, :] = jnp.zeros((SUBLANES, QK_WIDTH), jnp.float32)
            kh_ref[0:WINDOW, :] = jnp.zeros((WINDOW, A_WIDTH), jnp.bfloat16)
            vh_ref[0:WINDOW, :] = jnp.zeros((WINDOW, A_WIDTH), jnp.bfloat16)

    x = x_ref[0]
    inv = lax.rsqrt(jnp.mean(x * x, axis=-1, keepdims=True) + EPS)
    hb_ref[...] = _bf16(x * inv * ng_ref[...])
    for j in range(0, QK_WIDTH, PROJ_BLOCK):
        u_ref[SUBLANES:SUBLANES + tile, j:j + PROJ_BLOCK] = (
            _dot(hb_ref[...], w_ref[:, j:j + PROJ_BLOCK]) + b_ref[:, j:j + PROJ_BLOCK])

    def project(j):
        z_ref[:, j:j + PROJ_BLOCK] = (_dot(hb_ref[...], w_ref[:, QK_WIDTH + j:QK_WIDTH + j + PROJ_BLOCK])
                                      + b_ref[:, QK_WIDTH + j:QK_WIDTH + j + PROJ_BLOCK])

    for j in (Z_MV, Z_MO, Z_MZ):
        project(j)
    g_rows = lax.dot_general(wgt_ref[...], hb_ref[...], _NT,
                             preferred_element_type=jnp.float32) + bgt_ref[...]

    for j in range(0, QK_WIDTH, LANES):
        cs = slice(j, j + LANES)
        acc = cb_ref[:, cs] + cw_ref[CONV_WIDTH - 1:CONV_WIDTH, cs] * u_ref[SUBLANES:SUBLANES + tile, cs]
        for d in range(1, CONV_WIDTH):
            acc = acc + (cw_ref[CONV_WIDTH - 1 - d:CONV_WIDTH - d, cs]
                         * u_ref[SUBLANES - d:SUBLANES - d + tile, cs])
        act = _silu(acc)
        if j < M_WIDTH:
            qs_ref[:, cs] = act
        else:
            ks_ref[:, j - M_WIDTH:j - M_WIDTH + LANES] = act * (M_HEAD_DIM ** -0.5)

    @pl.when(t == n_tiles - 1)
    def _conv_out():
        conv_out_ref[0] = u_ref[tile:tile + SUBLANES, :]

    if n_tiles > 1:
        u_ref[0:SUBLANES, :] = u_ref[tile:tile + SUBLANES, :]

    row = lax.broadcasted_iota(jnp.int32, (1, chunk, chunk), 1)
    col = lax.broadcasted_iota(jnp.int32, (1, chunk, chunk), 2)
    lower = row >= col
    neg_inf = jnp.float32(-jnp.inf)

    def chunk_rows(r):
        return jnp.stack([r[:, c * chunk:(c + 1) * chunk] for c in range(n_chunks)], axis=0)

    def chunked(a):
        return a.reshape(n_chunks, chunk, a.shape[-1])

    def wide(a):
        return jnp.concatenate([a, a], axis=-1)

    tr = lax.broadcasted_iota(jnp.int32, (tile, tile), 0)
    tc = lax.broadcasted_iota(jnp.int32, (tile, tile), 1)
    same_chunk = jnp.bitwise_xor(tr, tc) < chunk
    cum_mat = jnp.where((tr <= tc) & same_chunk, 1.0, 0.0).astype(jnp.bfloat16)
    lf_all = _log_sigmoid(g_rows)
    hi, mid, lo3 = _split3(lf_all)
    b_all = _dot(hi, cum_mat) + _dot(mid, cum_mat) + _dot(lo3, cum_mat)
    lf_rows = lf_all[M_HEADS:2 * M_HEADS, :]
    a_rows = g_rows[0:M_HEADS, :] - b_all[M_HEADS:2 * M_HEADS, :]
    ones_v = jnp.ones((n_chunks, chunk, M_HEAD_DIM), jnp.bfloat16)

    for h in range(M_HEADS):
        cs = slice(h * M_HEAD_DIM, (h + 1) * M_HEAD_DIM)
        q = chunked(qs_ref[:, cs])
        k = chunked(ks_ref[:, cs])
        qb, kb = _bf16(q), _bf16(k)
        v_aug = jnp.concatenate(
            [_bf16(chunked(z_ref[:, Z_MV + h * M_HEAD_DIM:Z_MV + (h + 1) * M_HEAD_DIM])), ones_v], axis=2)
        lf_row = chunk_rows(lf_rows[h:h + 1, :])
        a_row = chunk_rows(a_rows[h:h + 1, :])
        b_col = jnp.sum(jnp.where(lower, lf_row, 0.0), axis=2, keepdims=True)
        m_loc = jnp.max(jnp.where(lower, a_row, neg_inf), axis=2, keepdims=True)
        decay = jnp.where(lower, jnp.exp(a_row - m_loc), 0.0)
        s = jnp.einsum("ctd,csd->cts", qb, kb, preferred_element_type=jnp.float32) * decay
        pv = jnp.einsum("cts,csd->ctd", _bf16(s), v_aug, preferred_element_type=jnp.float32)
        m_loc_end = m_loc[:, chunk - 1:chunk, :]
        b_end = b_col[:, chunk - 1:chunk, :]
        kw_t = _bf16(jnp.swapaxes(k, 1, 2) * decay[:, chunk - 1:chunk, :])
        kv = jnp.einsum("cks,csd->ckd", kw_t, v_aug, preferred_element_type=jnp.float32)
        cn_run = jnp.concatenate([c_ref[0, h], nd_ref[h]], axis=1)
        m_run = m_ref[h:h + 1, :]
        cn_start, m_start = [], []
        for c in range(n_chunks):
            cn_start.append(_bf16(cn_run))
            m_start.append(m_run)
            m_end = jnp.maximum(m_loc_end[c], m_run)
            cn_run = wide(jnp.exp(m_run - m_end)) * cn_run + wide(jnp.exp(m_loc_end[c] - m_end)) * kv[c]
            m_run = b_end[c] + m_end
        c_ref[0, h] = cn_run[:, 0:M_HEAD_DIM]
        nd_ref[h] = cn_run[:, M_HEAD_DIM:2 * M_HEAD_DIM]
        m_ref[h:h + 1, :] = m_run
        m0 = jnp.stack(m_start, axis=0)
        m_max = jnp.maximum(m_loc, m0)
        w_intra = jnp.exp(m_loc - m_max)
        w_state = jnp.exp(m0 - m_max)
        qc = jnp.einsum("ctk,ckv->ctv", qb, jnp.stack(cn_start, axis=0), preferred_element_type=jnp.float32)
        mix = wide(w_intra) * pv + wide(w_state) * qc
        num, den = mix[:, :, 0:M_HEAD_DIM], mix[:, :, M_HEAD_DIM:2 * M_HEAD_DIM]
        hh = num / jnp.maximum(jnp.abs(den), jnp.exp(-(b_col + m_max)))
        hn = hh * lax.rsqrt(jnp.mean(hh * hh, axis=2, keepdims=True) + EPS) * mg_ref[h:h + 1, :]
        hn = hn.reshape(tile, M_HEAD_DIM)
        hm = _sigmoid(z_ref[:, Z_MO + h * M_HEAD_DIM:Z_MO + (h + 1) * M_HEAD_DIM]) * hn
        gmix_ref[:, cs] = _bf16(hm * _silu(z_ref[:, Z_MZ + h * M_HEAD_DIM:Z_MZ + (h + 1) * M_HEAD_DIM]))
        project(Z_AQ + h * PROJ_BLOCK)

    @pl.when(t == n_tiles - 1)
    def _state_out():
        for h in range(M_HEADS):
            n_out_ref[0, h:h + 1, :] = nd_ref[h].T[0:1, :]
        m_out_ref[0] = m_ref[0:M_HEADS, :]

    lo = lax.broadcasted_iota(jnp.int32, (1, LANES), 1) < A_HEAD_DIM

    def head_norm(a, g):
        sq = a * a
        s_lo = jnp.sum(jnp.where(lo, sq, 0.0), axis=1, keepdims=True)
        s_hi = jnp.sum(jnp.where(lo, 0.0, sq), axis=1, keepdims=True)
        ms = jnp.where(lo, s_lo, s_hi) * (1.0 / A_HEAD_DIM)
        return a * lax.rsqrt(ms + EPS) * g

    if n_tiles > 1:
        ring = pl.multiple_of((t * tile) % WINDOW, tile)
    for p in range(A_PAIRS):
        cs = slice(p * LANES, (p + 1) * LANES)
        kn = head_norm(z_ref[:, Z_AK + p * LANES:Z_AK + (p + 1) * LANES], kg_ref[...])
        vv = z_ref[:, Z_AV + p * LANES:Z_AV + (p + 1) * LANES]
        kh_ref[WINDOW:WINDOW + tile, cs] = _bf16(kn)
        vh_ref[WINDOW:WINDOW + tile, cs] = _bf16(vv)
        if n_tiles > 1:
            k_out_ref[0, pl.ds(ring, tile), cs] = kn
            v_out_ref[0, pl.ds(ring, tile), cs] = vv
        else:
            k_out_ref[0, :, cs] = kn
            v_out_ref[0, :, cs] = vv

    if not has_state:
        kcol = lax.broadcasted_iota(jnp.int32, (n_chunks, 1, band), 2)
        first_valid = WINDOW - t * tile - chunk * lax.broadcasted_iota(jnp.int32, (n_chunks, 1, band), 0)
        visible = kcol >= first_valid
    bands = [slice(c * chunk, c * chunk + band) for c in range(n_chunks)]
    for p in range(A_PAIRS):
        cs = slice(p * LANES, (p + 1) * LANES)
        qn = head_norm(z_ref[:, Z_AQ + p * LANES:Z_AQ + (p + 1) * LANES], qg_ref[...])
        qn = chunked(qn * (A_HEAD_DIM ** -0.5))
        q2 = _bf16(jnp.concatenate([jnp.where(lo, qn, 0.0), jnp.where(lo, 0.0, qn)], axis=1))
        s = jnp.stack([lax.dot_general(q2[c], kh_ref[bands[c], cs], _NT, preferred_element_type=jnp.float32)
                       for c in range(n_chunks)], axis=0) + bias_ref[p]
        if not has_state:
            s = jnp.where(visible, s, neg_inf)
        e = jnp.exp(s - jnp.max(s, axis=2, keepdims=True))
        eb = _bf16(e)
        o2 = jnp.stack([_dot(eb[c], vh_ref[bands[c], cs]) for c in range(n_chunks)], axis=0)
        o2 = o2 / jnp.sum(e, axis=2, keepdims=True)
        o = jnp.where(lo, o2[:, 0:chunk], o2[:, chunk:2 * chunk]).reshape(tile, LANES)
        gatt_ref[:, cs] = _bf16(o * _silu(z_ref[:, Z_AZ + p * LANES:Z_AZ + (p + 1) * LANES]))
        project(Z_GM + p * PROJ_BLOCK)

    if n_tiles > 1:
        for r in range(0, WINDOW, LANES):
            kh_ref[r:r + LANES, :] = kh_ref[tile + r:tile + r + LANES, :]
            vh_ref[r:r + LANES, :] = vh_ref[tile + r:tile + r + LANES, :]

    for j in range(0, D_MODEL, PROJ_BLOCK):
        cs = slice(j, j + PROJ_BLOCK)
        u_m = _dot(gmix_ref[...], wbm_ref[:, cs])
        u_a = _dot(gatt_ref[...], wba_ref[:, cs])
        mix_ref[:, cs] = _bf16(_sigmoid(z_ref[:, Z_GM + j:Z_GM + j + PROJ_BLOCK]) * u_m
                               + _sigmoid(z_ref[:, Z_GA + j:Z_GA + j + PROJ_BLOCK]) * u_a)
    for j in range(0, D_MODEL, PROJ_BLOCK):
        cs = slice(j, j + PROJ_BLOCK)
        y_ref[0, :, cs] = x_ref[0, :, cs] + _dot(mix_ref[...], wout_ref[:, cs])


def _const_spec(shape):
    zeros = (0,) * len(shape)
    return pl.BlockSpec(shape, lambda b, t: zeros, pipeline_mode=pl.Buffered(1))


def _layer_call(x, state, params, *, tile, chunk):
    n_streams, seq, _ = x.shape
    n_tiles = seq // tile
    has_state = state is not None
    keep = min(WINDOW, seq)

    def per_stream(shape):
        nd = len(shape)
        return pl.BlockSpec((1,) + tuple(shape[1:]), lambda b, t: (b,) + (0,) * (nd - 1))

    in_specs = [pl.BlockSpec((1, tile, D_MODEL), lambda b, t: (b, t, 0))]
    args = [x]
    if has_state:
        for a in state:
            in_specs.append(per_stream(a.shape))
            args.append(a)
    for a in params:
        in_specs.append(_const_spec(a.shape))
        args.append(a)

    f32 = jnp.float32
    out_shape = (
        jax.ShapeDtypeStruct((n_streams, seq, D_MODEL), f32),
        jax.ShapeDtypeStruct((n_streams, M_HEADS, M_HEAD_DIM, M_HEAD_DIM), f32),
        jax.ShapeDtypeStruct((n_streams, M_HEADS, LANES), f32),
        jax.ShapeDtypeStruct((n_streams, M_HEADS, LANES), f32),
        jax.ShapeDtypeStruct((n_streams, SUBLANES, QK_WIDTH), f32),
        jax.ShapeDtypeStruct((n_streams, keep, A_WIDTH), f32),
        jax.ShapeDtypeStruct((n_streams, keep, A_WIDTH), f32),
    )
    out_specs = (pl.BlockSpec((1, tile, D_MODEL), lambda b, t: (b, t, 0)),) + tuple(
        per_stream(s.shape) for s in out_shape[1:])

    bf16 = jnp.bfloat16
    scratch = [
        pltpu.VMEM((tile, D_MODEL), bf16),
        pltpu.VMEM((tile + SUBLANES, QK_WIDTH), f32),
        pltpu.VMEM((tile, Z_WIDTH), f32),
        pltpu.VMEM((tile, M_WIDTH), f32),
        pltpu.VMEM((tile, M_WIDTH), f32),
        pltpu.VMEM((WINDOW + tile, A_WIDTH), bf16),
        pltpu.VMEM((WINDOW + tile, A_WIDTH), bf16),
        pltpu.VMEM((tile, M_WIDTH), bf16),
        pltpu.VMEM((tile, A_WIDTH), bf16),
        pltpu.VMEM((tile, D_MODEL), bf16),
        pltpu.VMEM((M_HEADS, M_HEAD_DIM, M_HEAD_DIM), f32),
        pltpu.VMEM((SUBLANES, LANES), f32),
    ]
    kern = functools.partial(_layer_kernel, tile=tile, chunk=chunk, n_tiles=n_tiles, has_state=has_state)
    return pl.pallas_call(
        kern,
        grid=(n_streams, n_tiles),
        in_specs=in_specs,
        out_specs=out_specs,
        out_shape=out_shape,
        scratch_shapes=scratch,
        compiler_params=pltpu.CompilerParams(
            dimension_semantics=("arbitrary", "arbitrary"),
            vmem_limit_bytes=VMEM_LIMIT_BYTES),
        name="layer_state" if has_state else "layer_fresh",
    )(*args)


def _pair_bias(rel_bias, chunk):
    assert chunk - 1 <= REL_CLIP <= WINDOW
    band = WINDOW + chunk
    n_far = chunk - 1 + WINDOW - REL_CLIP
    n_ext = band + chunk - 1
    rev = rel_bias[:, ::-1].astype(jnp.float32)
    ext = jnp.concatenate([jnp.broadcast_to(rev[:, :1], (A_HEADS, n_far)), rev[:, :n_ext - n_far]], axis=1)
    flat = jnp.tile(ext, (1, chunk))[:, chunk - 1:chunk - 1 + chunk * (n_ext - 1)]
    bias = flat.reshape(A_HEADS, chunk, n_ext - 1)[:, :, :band]
    return bias.reshape(A_PAIRS, 2 * chunk, band)


def _layer_params(norm_g, w_in, b_in, conv_w, conv_b, m_head_g, q_norm_g, k_norm_g, rel_bias,
                  w_bm, w_ba, w_out, chunk):
    points = [int(p) for p in np.cumsum(IN_WIDTHS)[:-1]]
    wq, wk, wv, wo, wz, wi, wf, waq, wak, wav, waz, wgm, wga = jnp.split(w_in, points, axis=1)
    bq, bk, bv, bo, bz, bi, bf, baq, bak, bav, baz, bgm, bga = jnp.split(b_in, points)
    w_gate = jnp.concatenate([wi, wf], axis=1)
    b_gate = jnp.concatenate([bi, bf])
    w_main = jnp.concatenate([wq, wk, wv, wo, wz, waq, wak, wav, waz, wgm, wga], axis=1).astype(jnp.bfloat16)
    b_main = jnp.concatenate([bq, bk, bv, bo, bz, baq, bak, bav, baz, bgm, bga])[None, :]
    return (norm_g[None, :], w_main, b_main, w_gate.T.astype(jnp.bfloat16), b_gate[:, None],
            conv_w, conv_b[None, :], m_head_g, jnp.tile(q_norm_g, 2)[None, :], jnp.tile(k_norm_g, 2)[None, :],
            _pair_bias(rel_bias, chunk), w_bm.astype(jnp.bfloat16), w_ba.astype(jnp.bfloat16),
            w_out.astype(jnp.bfloat16))


PROMPT_TILE = 256


def kernel(x_prompt, x_sample, state_mlstm_C, state_mlstm_n, state_mlstm_m, state_mlstm_conv,
           cache_attn_k, cache_attn_v, norm_g, w_in, b_in, conv_w, conv_b, m_head_g,
           q_norm_g, k_norm_g, rel_bias, w_bm, w_ba, w_out):
    depth = w_in.shape[0]
    xp, xs = x_prompt, x_sample
    n_p, t_p, _ = xp.shape
    n_s, t_s, _ = xs.shape
    outs_p, outs_s = [], []
    for l in range(depth):
        weights = (norm_g[l], w_in[l], b_in[l], conv_w[l], conv_b[l], m_head_g[l], q_norm_g[l],
                   k_norm_g[l], rel_bias[l], w_bm[l], w_ba[l], w_out[l])
        xp, c_p, n_p_, m_p, conv_p, k_p, v_p = _layer_call(
            xp, None, _layer_params(*weights, chunk=CHUNK), tile=PROMPT_TILE, chunk=CHUNK)
        outs_p.append((c_p, n_p_, m_p[:, :, 0], conv_p[:, SUBLANES - (CONV_WIDTH - 1):],
                       k_p.reshape(n_p, -1, A_HEADS, A_HEAD_DIM), v_p.reshape(n_p, -1, A_HEADS, A_HEAD_DIM)))
        state = (state_mlstm_C[l], state_mlstm_n[l],
                 jnp.broadcast_to(state_mlstm_m[l][:, :, None], (n_s, M_HEADS, LANES)),
                 jnp.pad(state_mlstm_conv[l], ((0, 0), (SUBLANES - (CONV_WIDTH - 1), 0), (0, 0))),
                 cache_attn_k[l].reshape(n_s, -1, A_WIDTH), cache_attn_v[l].reshape(n_s, -1, A_WIDTH))
        xs, c_s, n_s_, m_s, conv_s, k_s, v_s = _layer_call(
            xs, state, _layer_params(*weights, chunk=t_s), tile=t_s, chunk=t_s)
        outs_s.append((c_s, n_s_, m_s[:, :, 0], conv_s[:, SUBLANES - (CONV_WIDTH - 1):],
                       k_s.reshape(n_s, -1, A_HEADS, A_HEAD_DIM), v_s.reshape(n_s, -1, A_HEADS, A_HEAD_DIM)))
    stack = lambda outs, i: jnp.stack([o[i] for o in outs])
    return (xp, xs) + tuple(stack(outs_p, i) for i in range(6)) + tuple(stack(outs_s, i) for i in range(6))
```

```python
import functools

import jax
import jax.numpy as jnp
import numpy as np
from jax import lax
from jax.experimental import pallas as pl
from jax.experimental.pallas import tpu as pltpu

D_MODEL = 1024
CHUNK = 64
M_HEADS = 4
M_HEAD_DIM = 128
M_WIDTH = M_HEADS * M_HEAD_DIM
CONV_WIDTH = 4
A_HEADS = 8
A_HEAD_DIM = 64
A_WIDTH = A_HEADS * A_HEAD_DIM
A_PAIRS = A_HEADS // 2
WINDOW = 8 * CHUNK
REL_CLIP = 128
EPS = 1e-6
IN_WIDTHS = (M_WIDTH, M_WIDTH, M_WIDTH, M_WIDTH, M_WIDTH, M_HEADS, M_HEADS,
             A_WIDTH, A_WIDTH, A_WIDTH, A_WIDTH, D_MODEL, D_MODEL)

LANES = 128
SUBLANES = 8
VMEM_LIMIT_BYTES = 60 * 1024 * 1024

Z_MV = 0
Z_MO = Z_MV + M_WIDTH
Z_MZ = Z_MO + M_WIDTH
Z_AQ = Z_MZ + M_WIDTH
Z_AK = Z_AQ + A_WIDTH
Z_AV = Z_AK + A_WIDTH
Z_AZ = Z_AV + A_WIDTH
Z_GM = Z_AZ + A_WIDTH
Z_GA = Z_GM + D_MODEL
Z_WIDTH = Z_GA + D_MODEL
QK_WIDTH = 2 * M_WIDTH
W_A_WIDTH = QK_WIDTH + Z_AQ
PROJ_BLOCK = 512
N_Z_BLOCKS = Z_WIDTH // PROJ_BLOCK
CONV_GROUPS = 4

_NT = (((1,), (1,)), ((), ()))


def _sigmoid(x):
    return 1.0 / (1.0 + jnp.exp(-x))


def _silu(x):
    return x * _sigmoid(x)


def _log_sigmoid(x):
    return jnp.minimum(x, 0.0) - jnp.log1p(jnp.exp(-jnp.abs(x)))


def _bf16(x):
    return x.astype(jnp.bfloat16)


def _dot(a, b):
    return jnp.dot(a, b, preferred_element_type=jnp.float32)


def _split3(x):
    hi = _bf16(x)
    r = x - hi.astype(jnp.float32)
    mid = _bf16(r)
    lo = _bf16(r - mid.astype(jnp.float32))
    return hi, mid, lo


def _layer_kernel(*refs, tile, chunk, n_tiles, n_steps, has_state, pipelined):
    n_chunks = tile // chunk
    band = WINDOW + chunk
    it = iter(refs)
    x_ref = next(it)
    x0_ref = next(it) if pipelined else None
    if has_state:
        c0_ref, n0_ref, m0_ref, conv0_ref, k0_ref, v0_ref = (next(it) for _ in range(6))
    (ng_ref, wa_ref, wb_ref, b_ref, wgt_ref, bgt_ref, cw_ref, cb_ref, mg_ref, qg_ref, kg_ref,
     bias_ref, wbm_ref, wba_ref, wout_ref) = (next(it) for _ in range(15))
    y_ref, c_ref, n_out_ref, m_out_ref, conv_out_ref, k_out_ref, v_out_ref = (next(it) for _ in range(7))
    (hb_ref, u_ref, z2_ref, qs2_ref, ks2_ref, gates2_ref, xres2_ref, kh_ref, vh_ref, gmix_ref, gatt_ref,
     mix_ref, nd_ref, m_ref) = (next(it) for _ in range(14))

    i = pl.program_id(0)
    t2 = i % n_tiles
    if pipelined:
        t1 = jnp.minimum(i + 1, n_steps - 1) % n_tiles
        slot1, slot2 = (i + 1) % 2, i % 2
    else:
        t1, slot1, slot2 = t2, 0, 0

    neg_inf = jnp.float32(-jnp.inf)
    lo = lax.broadcasted_iota(jnp.int32, (1, LANES), 1) < A_HEAD_DIM

    def chunk_rows(r):
        return jnp.stack([r[:, c * chunk:(c + 1) * chunk] for c in range(n_chunks)], axis=0)

    def chunked(a):
        return a.reshape(n_chunks, chunk, a.shape[-1])

    def wide(a):
        return jnp.concatenate([a, a], axis=-1)

    def stage1(xin_ref, slot):
        z = z2_ref.at[slot]
        qs = qs2_ref.at[slot]
        ks = ks2_ref.at[slot]
        gates = gates2_ref.at[slot]
        xres = xres2_ref.at[slot]

        def norm():
            x = xin_ref[0]
            xres[...] = x
            inv = lax.rsqrt(jnp.mean(x * x, axis=-1, keepdims=True) + EPS)
            hb_ref[...] = _bf16(x * inv * ng_ref[...])

        def proj_u(j):
            u_ref[SUBLANES:SUBLANES + tile, j:j + PROJ_BLOCK] = (
                _dot(hb_ref[...], wa_ref[:, j:j + PROJ_BLOCK]) + b_ref[:, j:j + PROJ_BLOCK])

        def proj_z(j):
            if j < Z_AQ:
                w = wa_ref[:, QK_WIDTH + j:QK_WIDTH + j + PROJ_BLOCK]
            else:
                w = wb_ref[:, j - Z_AQ:j - Z_AQ + PROJ_BLOCK]
            z[:, j:j + PROJ_BLOCK] = _dot(hb_ref[...], w) + b_ref[:, QK_WIDTH + j:QK_WIDTH + j + PROJ_BLOCK]

        def gate_rows():
            g_rows = lax.dot_general(wgt_ref[...], hb_ref[...], _NT,
                                     preferred_element_type=jnp.float32) + bgt_ref[...]
            tr = lax.broadcasted_iota(jnp.int32, (tile, tile), 0)
            tc = lax.broadcasted_iota(jnp.int32, (tile, tile), 1)
            same_chunk = jnp.bitwise_xor(tr, tc) < chunk
            cum_mat = jnp.where((tr <= tc) & same_chunk, 1.0, 0.0).astype(jnp.bfloat16)
            lf_all = _log_sigmoid(g_rows)
            hi, mid, lo3 = _split3(lf_all)
            b_all = _dot(hi, cum_mat) + _dot(mid, cum_mat) + _dot(lo3, cum_mat)
            gates[...] = jnp.concatenate(
                [g_rows[0:M_HEADS, :] - b_all[M_HEADS:2 * M_HEADS, :], lf_all[M_HEADS:2 * M_HEADS, :]], axis=0)

        def conv(group):
            per = QK_WIDTH // LANES // CONV_GROUPS
            for j in range(group * per * LANES, (group + 1) * per * LANES, LANES):
                cs = slice(j, j + LANES)
                acc = cb_ref[:, cs] + cw_ref[CONV_WIDTH - 1:CONV_WIDTH, cs] * u_ref[SUBLANES:SUBLANES + tile, cs]
                for d in range(1, CONV_WIDTH):
                    acc = acc + (cw_ref[CONV_WIDTH - 1 - d:CONV_WIDTH - d, cs]
                                 * u_ref[SUBLANES - d:SUBLANES - d + tile, cs])
                act = _silu(acc)
                if j < M_WIDTH:
                    qs[:, cs] = act
                else:
                    ks[:, j - M_WIDTH:j - M_WIDTH + LANES] = act * (M_HEAD_DIM ** -0.5)

        return norm, proj_u, proj_z, gate_rows, conv

    def stage1_tail(t_tile, live):
        @pl.when((t_tile == n_tiles - 1) & live)
        def _conv_out():
            conv_out_ref[0] = u_ref[tile:tile + SUBLANES, :]

        if n_tiles > 1:
            u_ref[0:SUBLANES, :] = u_ref[tile:tile + SUBLANES, :]

    def stage1_all(xin_ref, slot):
        norm, proj_u, proj_z, gate_rows, conv = stage1(xin_ref, slot)
        norm()
        for j in range(0, QK_WIDTH, PROJ_BLOCK):
            proj_u(j)
        for j in range(0, Z_WIDTH, PROJ_BLOCK):
            proj_z(j)
        gate_rows()
        for g in range(CONV_GROUPS):
            conv(g)

    z = z2_ref.at[slot2]
    qs = qs2_ref.at[slot2]
    ks = ks2_ref.at[slot2]
    gates = gates2_ref.at[slot2]
    xres = xres2_ref.at[slot2]
    row = lax.broadcasted_iota(jnp.int32, (1, chunk, chunk), 1)
    col = lax.broadcasted_iota(jnp.int32, (1, chunk, chunk), 2)
    lower = row >= col
    ones_v = jnp.ones((n_chunks, chunk, M_HEAD_DIM), jnp.bfloat16)

    def mlstm_intra(h):
        cs = slice(h * M_HEAD_DIM, (h + 1) * M_HEAD_DIM)
        q = chunked(qs[:, cs])
        k = chunked(ks[:, cs])
        qb, kb = _bf16(q), _bf16(k)
        v_aug = jnp.concatenate(
            [_bf16(chunked(z[:, Z_MV + h * M_HEAD_DIM:Z_MV + (h + 1) * M_HEAD_DIM])), ones_v], axis=2)
        a_row = chunk_rows(gates[h:h + 1, :])
        lf_row = chunk_rows(gates[M_HEADS + h:M_HEADS + h + 1, :])
        b_col = jnp.sum(jnp.where(lower, lf_row, 0.0), axis=2, keepdims=True)
        m_loc = jnp.max(jnp.where(lower, a_row, neg_inf), axis=2, keepdims=True)
        decay = jnp.where(lower, jnp.exp(a_row - m_loc), 0.0)
        s = jnp.einsum("ctd,csd->cts", qb, kb, preferred_element_type=jnp.float32) * decay
        kw_t = _bf16(jnp.swapaxes(k, 1, 2) * decay[:, chunk - 1:chunk, :])
        return dict(h=h, qb=qb, v_aug=v_aug, b_col=b_col, m_loc=m_loc, s=_bf16(s), kw_t=kw_t)

    def mlstm_products(st):
        st["pv"] = jnp.einsum("cts,csd->ctd", st.pop("s"), st["v_aug"], preferred_element_type=jnp.float32)
        st["kv"] = jnp.einsum("cks,csd->ckd", st.pop("kw_t"), st.pop("v_aug"), preferred_element_type=jnp.float32)

    def mlstm_scan(st):
        h, m_loc, b_col, kv = st["h"], st["m_loc"], st["b_col"], st.pop("kv")
        m_loc_end = m_loc[:, chunk - 1:chunk, :]
        b_end = b_col[:, chunk - 1:chunk, :]
        cn_run = jnp.concatenate([c_ref[0, h], nd_ref[h]], axis=1)
        m_run = m_ref[h:h + 1, :]
        cn_start, m_start = [], []
        for c in range(n_chunks):
            cn_start.append(_bf16(cn_run))
            m_start.append(m_run)
            m_end = jnp.maximum(m_loc_end[c], m_run)
            cn_run = wide(jnp.exp(m_run - m_end)) * cn_run + wide(jnp.exp(m_loc_end[c] - m_end)) * kv[c]
            m_run = b_end[c] + m_end
        c_ref[0, h] = cn_run[:, 0:M_HEAD_DIM]
        nd_ref[h] = cn_run[:, M_HEAD_DIM:2 * M_HEAD_DIM]
        m_ref[h:h + 1, :] = m_run
        st["m0"] = jnp.stack(m_start, axis=0)
        st["qc"] = jnp.einsum("ctk,ckv->ctv", st.pop("qb"), jnp.stack(cn_start, axis=0),
                              preferred_element_type=jnp.float32)

    def mlstm_out(st):
        h, m_loc, b_col = st["h"], st["m_loc"], st["b_col"]
        cs = slice(h * M_HEAD_DIM, (h + 1) * M_HEAD_DIM)
        m_max = jnp.maximum(m_loc, st["m0"])
        w_intra = jnp.exp(m_loc - m_max)
        w_state = jnp.exp(st["m0"] - m_max)
        both = wide(w_intra) * st["pv"] + wide(w_state) * st["qc"]
        num, den = both[:, :, 0:M_HEAD_DIM], both[:, :, M_HEAD_DIM:2 * M_HEAD_DIM]
        hh = num / jnp.maximum(jnp.abs(den), jnp.exp(-(b_col + m_max)))
        hn = hh * lax.rsqrt(jnp.mean(hh * hh, axis=2, keepdims=True) + EPS) * mg_ref[h:h + 1, :]
        hn = hn.reshape(tile, M_HEAD_DIM)
        hm = _sigmoid(z[:, Z_MO + h * M_HEAD_DIM:Z_MO + (h + 1) * M_HEAD_DIM]) * hn
        gmix_ref[:, cs] = _bf16(hm * _silu(z[:, Z_MZ + h * M_HEAD_DIM:Z_MZ + (h + 1) * M_HEAD_DIM]))

    def mlstm(h):
        st = mlstm_intra(h)
        mlstm_products(st)
        mlstm_scan(st)
        mlstm_out(st)

    def head_norm(a, g):
        sq = a * a
        s_lo = jnp.sum(jnp.where(lo, sq, 0.0), axis=1, keepdims=True)
        s_hi = jnp.sum(jnp.where(lo, 0.0, sq), axis=1, keepdims=True)
        ms = jnp.where(lo, s_lo, s_hi) * (1.0 / A_HEAD_DIM)
        return a * lax.rsqrt(ms + EPS) * g

    def band_write(p):
        cs = slice(p * LANES, (p + 1) * LANES)
        kn = head_norm(z[:, Z_AK + p * LANES:Z_AK + (p + 1) * LANES], kg_ref[...])
        vv = z[:, Z_AV + p * LANES:Z_AV + (p + 1) * LANES]
        kh_ref[WINDOW:WINDOW + tile, cs] = _bf16(kn)
        vh_ref[WINDOW:WINDOW + tile, cs] = _bf16(vv)
        if n_tiles > 1:
            ring = pl.multiple_of((t2 * tile) % WINDOW, tile)
            k_out_ref[0, pl.ds(ring, tile), cs] = kn
            v_out_ref[0, pl.ds(ring, tile), cs] = vv
        else:
            k_out_ref[0, :, cs] = kn
            v_out_ref[0, :, cs] = vv

    bands = [slice(c * chunk, c * chunk + band) for c in range(n_chunks)]

    def attention_scores(p):
        cs = slice(p * LANES, (p + 1) * LANES)
        qn = head_norm(z[:, Z_AQ + p * LANES:Z_AQ + (p + 1) * LANES], qg_ref[...])
        qn = chunked(qn * (A_HEAD_DIM ** -0.5))
        q2 = _bf16(jnp.concatenate([jnp.where(lo, qn, 0.0), jnp.where(lo, 0.0, qn)], axis=1))
        s = jnp.stack([lax.dot_general(q2[c], kh_ref[bands[c], cs], _NT, preferred_element_type=jnp.float32)
                       for c in range(n_chunks)], axis=0) + bias_ref[p]
        if not has_state:
            kcol = lax.broadcasted_iota(jnp.int32, (n_chunks, 1, band), 2)
            first_valid = WINDOW - t2 * tile - chunk * lax.broadcasted_iota(jnp.int32, (n_chunks, 1, band), 0)
            s = jnp.where(kcol >= first_valid, s, neg_inf)
        e = jnp.exp(s - jnp.max(s, axis=2, keepdims=True))
        return p, _bf16(e), jnp.sum(e, axis=2, keepdims=True)

    def attention_values(p, eb, e_sum):
        cs = slice(p * LANES, (p + 1) * LANES)
        o2 = jnp.stack([_dot(eb[c], vh_ref[bands[c], cs]) for c in range(n_chunks)], axis=0)
        o2 = o2 / e_sum
        o = jnp.where(lo, o2[:, 0:chunk], o2[:, chunk:2 * chunk]).reshape(tile, LANES)
        gatt_ref[:, cs] = _bf16(o * _silu(z[:, Z_AZ + p * LANES:Z_AZ + (p + 1) * LANES]))

    def attention(p):
        attention_values(*attention_scores(p))

    def band_slide():
        for r in range(0, WINDOW, LANES):
            kh_ref[r:r + LANES, :] = kh_ref[tile + r:tile + r + LANES, :]
            vh_ref[r:r + LANES, :] = vh_ref[tile + r:tile + r + LANES, :]

    def merge(j):
        cs = slice(j, j + PROJ_BLOCK)
        u_m = _dot(gmix_ref[...], wbm_ref[:, cs])
        u_a = _dot(gatt_ref[...], wba_ref[:, cs])
        mix_ref[:, cs] = _bf16(_sigmoid(z[:, Z_GM + j:Z_GM + j + PROJ_BLOCK]) * u_m
                               + _sigmoid(z[:, Z_GA + j:Z_GA + j + PROJ_BLOCK]) * u_a)

    def out_proj(j):
        cs = slice(j, j + PROJ_BLOCK)
        y_ref[0, :, cs] = xres[:, cs] + _dot(mix_ref[...], wout_ref[:, cs])

    if pipelined:
        @pl.when(i == 0)
        def _prologue():
            u_ref[0:SUBLANES, :] = jnp.zeros((SUBLANES, QK_WIDTH), jnp.float32)
            stage1_all(x0_ref, 0)
            stage1_tail(0, True)

    @pl.when(t2 == 0)
    def _init_stream():
        if has_state:
            c_ref[0] = c0_ref[0]
            for h in range(M_HEADS):
                nd_ref[h] = jnp.broadcast_to(n0_ref[0, h:h + 1, :], (M_HEAD_DIM, M_HEAD_DIM)).T
            m_ref[0:M_HEADS, :] = m0_ref[0]
            kh_ref[0:WINDOW, :] = _bf16(k0_ref[0])
            vh_ref[0:WINDOW, :] = _bf16(v0_ref[0])
        else:
            c_ref[...] = jnp.zeros_like(c_ref)
            nd_ref[...] = jnp.zeros_like(nd_ref)
            m_ref[...] = jnp.zeros_like(m_ref)
            kh_ref[0:WINDOW, :] = jnp.zeros((WINDOW, A_WIDTH), jnp.bfloat16)
            vh_ref[0:WINDOW, :] = jnp.zeros((WINDOW, A_WIDTH), jnp.bfloat16)

    @pl.when(t1 == 0)
    def _init_conv_tail():
        if has_state:
            u_ref[0:SUBLANES, :] = conv0_ref[0]
        else:
            u_ref[0:SUBLANES, :] = jnp.zeros((SUBLANES, QK_WIDTH), jnp.float32)

    if pipelined:
        norm, proj_u, proj_z, gate_rows, conv = stage1(x_ref, slot1)
        blocks = iter(range(0, Z_WIDTH, PROJ_BLOCK))
        norm()
        proj_u(0)
        proj_u(PROJ_BLOCK)
        heads = [mlstm_intra(h) for h in range(M_HEADS)]
        proj_z(next(blocks))
        for st in heads:
            mlstm_products(st)
        proj_z(next(blocks))
        for st in heads:
            mlstm_scan(st)
        proj_z(next(blocks))
        gate_rows()
        for p in range(A_PAIRS):
            band_write(p)
        for st in heads:
            mlstm_out(st)
            proj_z(next(blocks))
        pending = attention_scores(0)
        for p in range(1, A_PAIRS):
            proj_z(next(blocks))
            nxt = attention_scores(p)
            attention_values(*pending)
            pending = nxt
        proj_z(next(blocks))
        attention_values(*pending)
        band_slide()
        merge(0)
        conv(0)
        merge(PROJ_BLOCK)
        conv(1)
        out_proj(0)
        conv(2)
        out_proj(PROJ_BLOCK)
        conv(3)
        assert next(blocks, None) is None
        stage1_tail(t1, i + 1 < n_steps)
    else:
        stage1_all(x_ref, slot1)
        stage1_tail(t1, True)
        for h in range(M_HEADS):
            mlstm(h)
        for p in range(A_PAIRS):
            band_write(p)
        for p in range(A_PAIRS):
            attention(p)
        if n_tiles > 1:
            band_slide()
        for j in range(0, D_MODEL, PROJ_BLOCK):
            merge(j)
        for j in range(0, D_MODEL, PROJ_BLOCK):
            out_proj(j)

    @pl.when(t2 == n_tiles - 1)
    def _state_out():
        for h in range(M_HEADS):
            n_out_ref[0, h:h + 1, :] = nd_ref[h].T[0:1, :]
        m_out_ref[0] = m_ref[0:M_HEADS, :]


def _const_spec(shape):
    zeros = (0,) * len(shape)
    return pl.BlockSpec(shape, lambda i: zeros, pipeline_mode=pl.Buffered(1))


def _layer_call(x, state, params, *, tile, chunk):
    n_streams, seq, _ = x.shape
    n_tiles = seq // tile
    n_steps = n_streams * n_tiles
    has_state = state is not None
    pipelined = not has_state and n_tiles > 1
    keep = min(WINDOW, seq)
    n_slots = 2 if pipelined else 1

    def per_stream(shape):
        nd = len(shape)
        return pl.BlockSpec((1,) + tuple(shape[1:]), lambda i: (i // n_tiles,) + (0,) * (nd - 1))

    def tile_spec(step_of):
        return pl.BlockSpec((1, tile, D_MODEL), lambda i: (step_of(i) // n_tiles, step_of(i) % n_tiles, 0))

    if pipelined:
        in_specs = [tile_spec(lambda i: jnp.minimum(i + 1, n_steps - 1)),
                    pl.BlockSpec((1, tile, D_MODEL), lambda i: (0, 0, 0), pipeline_mode=pl.Buffered(1))]
        args = [x, x]
    else:
        in_specs = [tile_spec(lambda i: i)]
        args = [x]
    if has_state:
        for a in state:
            in_specs.append(per_stream(a.shape))
            args.append(a)
    for a in params:
        in_specs.append(_const_spec(a.shape))
        args.append(a)

    f32 = jnp.float32
    out_shape = (
        jax.ShapeDtypeStruct((n_streams, seq, D_MODEL), f32),
        jax.ShapeDtypeStruct((n_streams, M_HEADS, M_HEAD_DIM, M_HEAD_DIM), f32),
        jax.ShapeDtypeStruct((n_streams, M_HEADS, LANES), f32),
        jax.ShapeDtypeStruct((n_streams, M_HEADS, LANES), f32),
        jax.ShapeDtypeStruct((n_streams, SUBLANES, QK_WIDTH), f32),
        jax.ShapeDtypeStruct((n_streams, keep, A_WIDTH), f32),
        jax.ShapeDtypeStruct((n_streams, keep, A_WIDTH), f32),
    )
    out_specs = (tile_spec(lambda i: i),) + tuple(per_stream(s.shape) for s in out_shape[1:])

    bf16 = jnp.bfloat16
    scratch = [
        pltpu.VMEM((tile, D_MODEL), bf16),
        pltpu.VMEM((tile + SUBLANES, QK_WIDTH), f32),
        pltpu.VMEM((n_slots, tile, Z_WIDTH), f32),
        pltpu.VMEM((n_slots, tile, M_WIDTH), f32),
        pltpu.VMEM((n_slots, tile, M_WIDTH), f32),
        pltpu.VMEM((n_slots, 2 * M_HEADS, tile), f32),
        pltpu.VMEM((n_slots, tile, D_MODEL), f32),
        pltpu.VMEM((WINDOW + tile, A_WIDTH), bf16),
        pltpu.VMEM((WINDOW + tile, A_WIDTH), bf16),
        pltpu.VMEM((tile, M_WIDTH), bf16),
        pltpu.VMEM((tile, A_WIDTH), bf16),
        pltpu.VMEM((tile, D_MODEL), bf16),
        pltpu.VMEM((M_HEADS, M_HEAD_DIM, M_HEAD_DIM), f32),
        pltpu.VMEM((SUBLANES, LANES), f32),
    ]
    kern = functools.partial(_layer_kernel, tile=tile, chunk=chunk, n_tiles=n_tiles, n_steps=n_steps,
                             has_state=has_state, pipelined=pipelined)
    return pl.pallas_call(
        kern,
        grid=(n_steps,),
        in_specs=in_specs,
        out_specs=out_specs,
        out_shape=out_shape,
        scratch_shapes=scratch,
        compiler_params=pltpu.CompilerParams(
            dimension_semantics=("arbitrary",),
            vmem_limit_bytes=VMEM_LIMIT_BYTES),
        name="layer_state" if has_state else "layer_fresh",
    )(*args)


def _pair_bias(rel_bias, chunk):
    assert chunk - 1 <= REL_CLIP <= WINDOW
    band = WINDOW + chunk
    n_far = chunk - 1 + WINDOW - REL_CLIP
    n_ext = band + chunk - 1
    rev = rel_bias[:, ::-1].astype(jnp.float32)
    ext = jnp.concatenate([jnp.broadcast_to(rev[:, :1], (A_HEADS, n_far)), rev[:, :n_ext - n_far]], axis=1)
    flat = jnp.tile(ext, (1, chunk))[:, chunk - 1:chunk - 1 + chunk * (n_ext - 1)]
    bias = flat.reshape(A_HEADS, chunk, n_ext - 1)[:, :, :band]
    return bias.reshape(A_PAIRS, 2 * chunk, band)


def _layer_params(norm_g, w_in, b_in, conv_w, conv_b, m_head_g, q_norm_g, k_norm_g, rel_bias,
                  w_bm, w_ba, w_out, chunk):
    gate_lo = W_A_WIDTH
    gate_hi = gate_lo + 2 * M_HEADS
    w_a = w_in[:, :gate_lo].astype(jnp.bfloat16)
    w_b = w_in[:, gate_hi:].astype(jnp.bfloat16)
    b_main = jnp.concatenate([b_in[:gate_lo], b_in[gate_hi:]])[None, :]
    w_gate_t = w_in[:, gate_lo:gate_hi].T.astype(jnp.bfloat16)
    return (norm_g[None, :], w_a, w_b, b_main, w_gate_t, b_in[gate_lo:gate_hi, None],
            conv_w, conv_b[None, :], m_head_g, jnp.tile(q_norm_g, 2)[None, :], jnp.tile(k_norm_g, 2)[None, :],
            _pair_bias(rel_bias, chunk), w_bm.astype(jnp.bfloat16), w_ba.astype(jnp.bfloat16),
            w_out.astype(jnp.bfloat16))


PROMPT_TILE = 256


def kernel(x_prompt, x_sample, state_mlstm_C, state_mlstm_n, state_mlstm_m, state_mlstm_conv,
           cache_attn_k, cache_attn_v, norm_g, w_in, b_in, conv_w, conv_b, m_head_g,
           q_norm_g, k_norm_g, rel_bias, w_bm, w_ba, w_out):
    assert tuple(int(w) for w in IN_WIDTHS[:5]) == (M_WIDTH,) * 5 and sum(IN_WIDTHS) == w_in.shape[-1]
    depth = w_in.shape[0]
    xp, xs = x_prompt, x_sample
    n_p, t_p, _ = xp.shape
    n_s, t_s, _ = xs.shape
    outs_p, outs_s = [], []
    for l in range(depth):
        weights = (norm_g[l], w_in[l], b_in[l], conv_w[l], conv_b[l], m_head_g[l], q_norm_g[l],
                   k_norm_g[l], rel_bias[l], w_bm[l], w_ba[l], w_out[l])
        xp, c_p, n_p_, m_p, conv_p, k_p, v_p = _layer_call(
            xp, None, _layer_params(*weights, chunk=CHUNK), tile=PROMPT_TILE, chunk=CHUNK)
        outs_p.append((c_p, n_p_, m_p[:, :, 0], conv_p[:, SUBLANES - (CONV_WIDTH - 1):],
                       k_p.reshape(n_p, -1, A_HEADS, A_HEAD_DIM), v_p.reshape(n_p, -1, A_HEADS, A_HEAD_DIM)))
        state = (state_mlstm_C[l], state_mlstm_n[l],
                 jnp.broadcast_to(state_mlstm_m[l][:, :, None], (n_s, M_HEADS, LANES)),
                 jnp.pad(state_mlstm_conv[l], ((0, 0), (SUBLANES - (CONV_WIDTH - 1), 0), (0, 0))),
                 cache_attn_k[l].reshape(n_s, -1, A_WIDTH), cache_attn_v[l].reshape(n_s, -1, A_WIDTH))
        xs, c_s, n_s_, m_s, conv_s, k_s, v_s = _layer_call(
            xs, state, _layer_params(*weights, chunk=t_s), tile=t_s, chunk=t_s)
        outs_s.append((c_s, n_s_, m_s[:, :, 0], conv_s[:, SUBLANES - (CONV_WIDTH - 1):],
                       k_s.reshape(n_s, -1, A_HEADS, A_HEAD_DIM), v_s.reshape(n_s, -1, A_HEADS, A_HEAD_DIM)))
    stack = lambda outs, i: jnp.stack([o[i] for o in outs])
    return (xp, xs) + tuple(stack(outs_p, i) for i in range(6)) + tuple(stack(outs_s, i) for i in range(6))
```

```python
import functools

import jax
import jax.numpy as jnp
import numpy as np
from jax import lax
from jax.experimental import pallas as pl
from jax.experimental.pallas import tpu as pltpu

D_MODEL = 1024
CHUNK = 64
M_HEADS = 4
M_HEAD_DIM = 128
M_WIDTH = M_HEADS * M_HEAD_DIM
CONV_WIDTH = 4
A_HEADS = 8
A_HEAD_DIM = 64
A_WIDTH = A_HEADS * A_HEAD_DIM
A_PAIRS = A_HEADS // 2
WINDOW = 8 * CHUNK
REL_CLIP = 128
BIAS_FAR = WINDOW - REL_CLIP
EPS = 1e-6
IN_WIDTHS = (M_WIDTH, M_WIDTH, M_WIDTH, M_WIDTH, M_WIDTH, M_HEADS, M_HEADS,
             A_WIDTH, A_WIDTH, A_WIDTH, A_WIDTH, D_MODEL, D_MODEL)

LANES = 128
SUBLANES = 8
VMEM_LIMIT_BYTES = 60 * 1024 * 1024

Z_MV = 0
Z_MO = Z_MV + M_WIDTH
Z_MZ = Z_MO + M_WIDTH
Z_AQ = Z_MZ + M_WIDTH
Z_AK = Z_AQ + A_WIDTH
Z_AV = Z_AK + A_WIDTH
Z_AZ = Z_AV + A_WIDTH
Z_GM = Z_AZ + A_WIDTH
Z_GA = Z_GM + D_MODEL
Z_WIDTH = Z_GA + D_MODEL
QK_WIDTH = 2 * M_WIDTH
W_A_WIDTH = QK_WIDTH + Z_AQ
PROJ_BLOCK = 512
CONV_GROUPS = 4

_NT = (((1,), (1,)), ((), ()))


def _sigmoid(x):
    return 1.0 / (1.0 + jnp.exp(-x))


def _silu(x):
    return x * _sigmoid(x)


def _log_sigmoid(x):
    return jnp.minimum(x, 0.0) - jnp.log1p(jnp.exp(-jnp.abs(x)))


def _bf16(x):
    return x.astype(jnp.bfloat16)


def _dot(a, b):
    return jnp.dot(a, b, preferred_element_type=jnp.float32)


def _split3(x):
    hi = _bf16(x)
    r = x - hi.astype(jnp.float32)
    mid = _bf16(r)
    lo = _bf16(r - mid.astype(jnp.float32))
    return hi, mid, lo


def _stage1_pieces(rows, chunk, hb_ref, ng_ref, wa_ref, wb_ref, b_ref, wgt_ref, bgt_ref):
    def norm(x):
        inv = lax.rsqrt(jnp.mean(x * x, axis=-1, keepdims=True) + EPS)
        hb_ref[...] = _bf16(x * inv * ng_ref[...])

    def proj_u(j, u_dst):
        u_dst[:, j:j + PROJ_BLOCK] = _dot(hb_ref[...], wa_ref[:, j:j + PROJ_BLOCK]) + b_ref[:, j:j + PROJ_BLOCK]

    def proj_z(j, z_dst):
        if j < Z_AQ:
            w = wa_ref[:, QK_WIDTH + j:QK_WIDTH + j + PROJ_BLOCK]
        else:
            w = wb_ref[:, j - Z_AQ:j - Z_AQ + PROJ_BLOCK]
        z_dst[:, j:j + PROJ_BLOCK] = _dot(hb_ref[...], w) + b_ref[:, QK_WIDTH + j:QK_WIDTH + j + PROJ_BLOCK]

    def gate_rows(gates_dst):
        g_rows = lax.dot_general(wgt_ref[...], hb_ref[...], _NT,
                                 preferred_element_type=jnp.float32) + bgt_ref[...]
        tr = lax.broadcasted_iota(jnp.int32, (rows, rows), 0)
        tc = lax.broadcasted_iota(jnp.int32, (rows, rows), 1)
        same_chunk = jnp.bitwise_xor(tr, tc) < chunk
        cum_mat = jnp.where((tr <= tc) & same_chunk, 1.0, 0.0).astype(jnp.bfloat16)
        lf_all = _log_sigmoid(g_rows)
        hi, mid, lo3 = _split3(lf_all)
        b_all = _dot(hi, cum_mat) + _dot(mid, cum_mat) + _dot(lo3, cum_mat)
        gates_dst[...] = jnp.concatenate(
            [g_rows[0:M_HEADS, :] - b_all[M_HEADS:2 * M_HEADS, :], lf_all[M_HEADS:2 * M_HEADS, :]], axis=0)

    return norm, proj_u, proj_z, gate_rows


def _merge_block(j, z, gmix, gatt, mix_ref, wbm_ref, wba_ref):
    cs = slice(j, j + PROJ_BLOCK)
    u_m = _dot(gmix[...], wbm_ref[:, cs])
    u_a = _dot(gatt[...], wba_ref[:, cs])
    mix_ref[:, cs] = _bf16(_sigmoid(z[:, Z_GM + j:Z_GM + j + PROJ_BLOCK]) * u_m
                           + _sigmoid(z[:, Z_GA + j:Z_GA + j + PROJ_BLOCK]) * u_a)


def _out_block(j, xres, mix_ref, wout_ref, y_dst):
    cs = slice(j, j + PROJ_BLOCK)
    y_dst[:, cs] = xres[:, cs] + _dot(mix_ref[...], wout_ref[:, cs])


def _layer_kernel(*refs, tile, chunk, n_tiles, n_steps, pipelined):
    has_state = not pipelined
    n_chunks = tile // chunk
    band = WINDOW + chunk
    it = iter(refs)
    if pipelined:
        x_ref, x0_ref = next(it), next(it)
        ng_ref, wa_ref, wb_ref, b_ref, wgt_ref, bgt_ref = (next(it) for _ in range(6))
    else:
        u_in_ref, z_in_ref, gates_in_ref = (next(it) for _ in range(3))
        c0_ref, n0_ref, m0_ref, conv0_ref, k0_ref, v0_ref = (next(it) for _ in range(6))
    cw_ref, cb_ref, mg_ref, qg_ref, kg_ref, bias_ref = (next(it) for _ in range(6))
    if pipelined:
        wbm_ref, wba_ref, wout_ref = (next(it) for _ in range(3))
        y_ref = next(it)
    else:
        gmix_out_ref, gatt_out_ref = next(it), next(it)
    c_ref, n_out_ref, m_out_ref, conv_out_ref, k_out_ref, v_out_ref = (next(it) for _ in range(6))
    if pipelined:
        hb_ref, u_ref, z2_ref, qs2_ref, ks2_ref, gates2_ref, xres2_ref = (next(it) for _ in range(7))
        kh_ref, vh_ref, gmix_ref, gatt_ref, mix_ref, nd_ref, m_ref = (next(it) for _ in range(7))
    else:
        u_ref, qs_ref, ks_ref, kh_ref, vh_ref, nd_ref, m_ref = (next(it) for _ in range(7))

    i = pl.program_id(0)
    t2 = i % n_tiles
    if pipelined:
        t1 = jnp.minimum(i + 1, n_steps - 1) % n_tiles
        slot1, slot2 = (i + 1) % 2, i % 2
        z, qs, ks = z2_ref.at[slot2], qs2_ref.at[slot2], ks2_ref.at[slot2]
        gates, xres = gates2_ref.at[slot2], xres2_ref.at[slot2]
        gmix, gatt = gmix_ref, gatt_ref
    else:
        z, qs, ks, gates = z_in_ref.at[0], qs_ref, ks_ref, gates_in_ref.at[0]
        gmix, gatt = gmix_out_ref.at[0], gatt_out_ref.at[0]

    neg_inf = jnp.float32(-jnp.inf)
    lo = lax.broadcasted_iota(jnp.int32, (1, LANES), 1) < A_HEAD_DIM

    def chunk_rows(r):
        return jnp.stack([r[:, c * chunk:(c + 1) * chunk] for c in range(n_chunks)], axis=0)

    def chunked(a):
        return a.reshape(n_chunks, chunk, a.shape[-1])

    def wide(a):
        return jnp.concatenate([a, a], axis=-1)

    def conv(group, qs_dst, ks_dst):
        per = QK_WIDTH // LANES // CONV_GROUPS
        for j in range(group * per * LANES, (group + 1) * per * LANES, LANES):
            cs = slice(j, j + LANES)
            acc = cb_ref[:, cs] + cw_ref[CONV_WIDTH - 1:CONV_WIDTH, cs] * u_ref[SUBLANES:SUBLANES + tile, cs]
            for d in range(1, CONV_WIDTH):
                acc = acc + (cw_ref[CONV_WIDTH - 1 - d:CONV_WIDTH - d, cs]
                             * u_ref[SUBLANES - d:SUBLANES - d + tile, cs])
            act = _silu(acc)
            if j < M_WIDTH:
                qs_dst[:, cs] = act
            else:
                ks_dst[:, j - M_WIDTH:j - M_WIDTH + LANES] = act * (M_HEAD_DIM ** -0.5)

    def conv_tail(t_tile, live):
        @pl.when((t_tile == n_tiles - 1) & live)
        def _conv_out():
            conv_out_ref[0] = u_ref[tile:tile + SUBLANES, :]

        if n_tiles > 1:
            u_ref[0:SUBLANES, :] = u_ref[tile:tile + SUBLANES, :]

    row = lax.broadcasted_iota(jnp.int32, (1, chunk, chunk), 1)
    col = lax.broadcasted_iota(jnp.int32, (1, chunk, chunk), 2)
    lower = row >= col
    ones_v = jnp.ones((n_chunks, chunk, M_HEAD_DIM), jnp.bfloat16)

    def mlstm_intra(h):
        cs = slice(h * M_HEAD_DIM, (h + 1) * M_HEAD_DIM)
        q = chunked(qs[:, cs])
        k = chunked(ks[:, cs])
        qb, kb = _bf16(q), _bf16(k)
        v_aug = jnp.concatenate(
            [_bf16(chunked(z[:, Z_MV + h * M_HEAD_DIM:Z_MV + (h + 1) * M_HEAD_DIM])), ones_v], axis=2)
        a_row = chunk_rows(gates[h:h + 1, :])
        lf_row = chunk_rows(gates[M_HEADS + h:M_HEADS + h + 1, :])
        b_col = jnp.sum(jnp.where(lower, lf_row, 0.0), axis=2, keepdims=True)
        m_loc = jnp.max(jnp.where(lower, a_row, neg_inf), axis=2, keepdims=True)
        decay = jnp.where(lower, jnp.exp(a_row - m_loc), 0.0)
        s = jnp.einsum("ctd,csd->cts", qb, kb, preferred_element_type=jnp.float32) * decay
        kw_t = _bf16(jnp.swapaxes(k, 1, 2) * decay[:, chunk - 1:chunk, :])
        return dict(h=h, qb=qb, v_aug=v_aug, b_col=b_col, m_loc=m_loc, s=_bf16(s), kw_t=kw_t)

    def mlstm_products(st):
        st["pv"] = jnp.einsum("cts,csd->ctd", st.pop("s"), st["v_aug"], preferred_element_type=jnp.float32)
        st["kv"] = jnp.einsum("cks,csd->ckd", st.pop("kw_t"), st.pop("v_aug"), preferred_element_type=jnp.float32)

    def mlstm_scan(st):
        h, m_loc, b_col, kv = st["h"], st["m_loc"], st["b_col"], st.pop("kv")
        m_loc_end = m_loc[:, chunk - 1:chunk, :]
        b_end = b_col[:, chunk - 1:chunk, :]
        cn_run = jnp.concatenate([c_ref[0, h], nd_ref[h]], axis=1)
        m_run = m_ref[h:h + 1, :]
        cn_start, m_start = [], []
        for c in range(n_chunks):
            cn_start.append(_bf16(cn_run))
            m_start.append(m_run)
            m_end = jnp.maximum(m_loc_end[c], m_run)
            cn_run = wide(jnp.exp(m_run - m_end)) * cn_run + wide(jnp.exp(m_loc_end[c] - m_end)) * kv[c]
            m_run = b_end[c] + m_end
        c_ref[0, h] = cn_run[:, 0:M_HEAD_DIM]
        nd_ref[h] = cn_run[:, M_HEAD_DIM:2 * M_HEAD_DIM]
        m_ref[h:h + 1, :] = m_run
        st["m0"] = jnp.stack(m_start, axis=0)
        st["qc"] = jnp.einsum("ctk,ckv->ctv", st.pop("qb"), jnp.stack(cn_start, axis=0),
                              preferred_element_type=jnp.float32)

    def mlstm_out(st):
        h, m_loc, b_col = st["h"], st["m_loc"], st["b_col"]
        cs = slice(h * M_HEAD_DIM, (h + 1) * M_HEAD_DIM)
        m_max = jnp.maximum(m_loc, st["m0"])
        w_intra = jnp.exp(m_loc - m_max)
        w_state = jnp.exp(st["m0"] - m_max)
        both = wide(w_intra) * st["pv"] + wide(w_state) * st["qc"]
        num, den = both[:, :, 0:M_HEAD_DIM], both[:, :, M_HEAD_DIM:2 * M_HEAD_DIM]
        hh = num / jnp.maximum(jnp.abs(den), jnp.exp(-(b_col + m_max)))
        hn = hh * lax.rsqrt(jnp.mean(hh * hh, axis=2, keepdims=True) + EPS) * mg_ref[h:h + 1, :]
        hn = hn.reshape(tile, M_HEAD_DIM)
        hm = _sigmoid(z[:, Z_MO + h * M_HEAD_DIM:Z_MO + (h + 1) * M_HEAD_DIM]) * hn
        gmix[:, cs] = _bf16(hm * _silu(z[:, Z_MZ + h * M_HEAD_DIM:Z_MZ + (h + 1) * M_HEAD_DIM]))

    def head_norm(a, g):
        sq = a * a
        s_lo = jnp.sum(jnp.where(lo, sq, 0.0), axis=1, keepdims=True)
        s_hi = jnp.sum(jnp.where(lo, 0.0, sq), axis=1, keepdims=True)
        ms = jnp.where(lo, s_lo, s_hi) * (1.0 / A_HEAD_DIM)
        return a * lax.rsqrt(ms + EPS) * g

    def band_write(p):
        cs = slice(p * LANES, (p + 1) * LANES)
        kn = head_norm(z[:, Z_AK + p * LANES:Z_AK + (p + 1) * LANES], kg_ref[...])
        vv = z[:, Z_AV + p * LANES:Z_AV + (p + 1) * LANES]
        kh_ref[WINDOW:WINDOW + tile, cs] = _bf16(kn)
        vh_ref[WINDOW:WINDOW + tile, cs] = _bf16(vv)
        if n_tiles > 1:
            ring = pl.multiple_of((t2 * tile) % WINDOW, tile)
            k_out_ref[0, pl.ds(ring, tile), cs] = kn
            v_out_ref[0, pl.ds(ring, tile), cs] = vv
        else:
            k_out_ref[0, :, cs] = kn
            v_out_ref[0, :, cs] = vv

    bands = [slice(c * chunk, c * chunk + band) for c in range(n_chunks)]

    def attention_scores(p):
        cs = slice(p * LANES, (p + 1) * LANES)
        qn = head_norm(z[:, Z_AQ + p * LANES:Z_AQ + (p + 1) * LANES], qg_ref[...])
        qn = chunked(qn * (A_HEAD_DIM ** -0.5))
        q2 = _bf16(jnp.concatenate([jnp.where(lo, qn, 0.0), jnp.where(lo, 0.0, qn)], axis=1))
        s = jnp.stack([lax.dot_general(q2[c], kh_ref[bands[c], cs], _NT, preferred_element_type=jnp.float32)
                       for c in range(n_chunks)], axis=0)
        s = jnp.concatenate([s[:, :, 0:BIAS_FAR], s[:, :, BIAS_FAR:band] + bias_ref[p]], axis=2)
        if not has_state:
            kcol = lax.broadcasted_iota(jnp.int32, (n_chunks, 1, band), 2)
            first_valid = WINDOW - t2 * tile - chunk * lax.broadcasted_iota(jnp.int32, (n_chunks, 1, band), 0)
            s = jnp.where(kcol >= first_valid, s, neg_inf)
        e = jnp.exp(s - jnp.max(s, axis=2, keepdims=True))
        return p, _bf16(e), jnp.sum(e, axis=2, keepdims=True)

    def attention_values(p, eb, e_sum):
        cs = slice(p * LANES, (p + 1) * LANES)
        o2 = jnp.stack([_dot(eb[c], vh_ref[bands[c], cs]) for c in range(n_chunks)], axis=0)
        o2 = o2 / e_sum
        o = jnp.where(lo, o2[:, 0:chunk], o2[:, chunk:2 * chunk]).reshape(tile, LANES)
        gatt[:, cs] = _bf16(o * _silu(z[:, Z_AZ + p * LANES:Z_AZ + (p + 1) * LANES]))

    def attention_all(fillers):
        fillers = list(fillers)
        pending = None
        for p in range(A_PAIRS):
            nxt = attention_scores(p)
            if pending is not None:
                attention_values(*pending)
            pending = nxt
            if fillers:
                fillers.pop(0)()
        attention_values(*pending)
        for f in fillers:
            f()

    def band_slide():
        for r in range(0, WINDOW, LANES):
            kh_ref[r:r + LANES, :] = kh_ref[tile + r:tile + r + LANES, :]
            vh_ref[r:r + LANES, :] = vh_ref[tile + r:tile + r + LANES, :]

    if pipelined:
        norm, proj_u, proj_z, gate_rows = _stage1_pieces(
            tile, chunk, hb_ref, ng_ref, wa_ref, wb_ref, b_ref, wgt_ref, bgt_ref)
        u_new = u_ref.at[SUBLANES:SUBLANES + tile]

        @pl.when(i == 0)
        def _prologue():
            u_ref[0:SUBLANES, :] = jnp.zeros((SUBLANES, QK_WIDTH), jnp.float32)
            x0 = x0_ref[0]
            xres2_ref[0] = x0
            norm(x0)
            for j in range(0, QK_WIDTH, PROJ_BLOCK):
                proj_u(j, u_new)
            for j in range(0, Z_WIDTH, PROJ_BLOCK):
                proj_z(j, z2_ref.at[0])
            gate_rows(gates2_ref.at[0])
            for g in range(CONV_GROUPS):
                conv(g, qs2_ref.at[0], ks2_ref.at[0])
            conv_tail(0, True)

    @pl.when(t2 == 0)
    def _init_stream():
        if has_state:
            c_ref[0] = c0_ref[0]
            for h in range(M_HEADS):
                nd_ref[h] = jnp.broadcast_to(n0_ref[0, h:h + 1, :], (M_HEAD_DIM, M_HEAD_DIM)).T
            m_ref[0:M_HEADS, :] = m0_ref[0]
            kh_ref[0:WINDOW, :] = _bf16(k0_ref[0])
            vh_ref[0:WINDOW, :] = _bf16(v0_ref[0])
        else:
            c_ref[...] = jnp.zeros_like(c_ref)
            nd_ref[...] = jnp.zeros_like(nd_ref)
            m_ref[...] = jnp.zeros_like(m_ref)
            kh_ref[0:WINDOW, :] = jnp.zeros((WINDOW, A_WIDTH), jnp.bfloat16)
            vh_ref[0:WINDOW, :] = jnp.zeros((WINDOW, A_WIDTH), jnp.bfloat16)

    if pipelined:
        @pl.when(t1 == 0)
        def _init_conv_tail():
            u_ref[0:SUBLANES, :] = jnp.zeros((SUBLANES, QK_WIDTH), jnp.float32)

        z1, qs1, ks1 = z2_ref.at[slot1], qs2_ref.at[slot1], ks2_ref.at[slot1]
        blocks = iter(range(0, Z_WIDTH, PROJ_BLOCK))
        x1 = x_ref[0]
        xres2_ref[slot1] = x1
        norm(x1)
        proj_u(0, u_new)
        proj_u(PROJ_BLOCK, u_new)
        heads = [mlstm_intra(h) for h in range(M_HEADS)]
        proj_z(next(blocks), z1)
        for st in heads:
            mlstm_products(st)
        proj_z(next(blocks), z1)
        for st in heads:
            mlstm_scan(st)
        proj_z(next(blocks), z1)
        gate_rows(gates2_ref.at[slot1])
        for p in range(A_PAIRS):
            band_write(p)
        for st in heads:
            mlstm_out(st)
            proj_z(next(blocks), z1)
        attention_all([functools.partial(proj_z, j, z1) for j in blocks])
        band_slide()
        for g, j in enumerate(range(0, D_MODEL, PROJ_BLOCK)):
            _merge_block(j, z, gmix, gatt, mix_ref, wbm_ref, wba_ref)
            conv(g, qs1, ks1)
        for g, j in enumerate(range(0, D_MODEL, PROJ_BLOCK)):
            _out_block(j, xres, mix_ref, wout_ref, y_ref.at[0])
            conv(D_MODEL // PROJ_BLOCK + g, qs1, ks1)
        conv_tail(t1, i + 1 < n_steps)
    else:
        u_ref[0:SUBLANES, :] = conv0_ref[0]
        u_ref[SUBLANES:SUBLANES + tile, :] = u_in_ref[0]
        for g in range(CONV_GROUPS):
            conv(g, qs, ks)
        conv_tail(t2, True)
        heads = [mlstm_intra(h) for h in range(M_HEADS)]
        for st in heads:
            mlstm_products(st)
        for st in heads:
            mlstm_scan(st)
        for p in range(A_PAIRS):
            band_write(p)
        for st in heads:
            mlstm_out(st)
        attention_all([])

    @pl.when(t2 == n_tiles - 1)
    def _state_out():
        for h in range(M_HEADS):
            n_out_ref[0, h:h + 1, :] = nd_ref[h].T[0:1, :]
        m_out_ref[0] = m_ref[0:M_HEADS, :]


def _project_kernel(x_ref, ng_ref, wa_ref, wb_ref, b_ref, wgt_ref, bgt_ref, u_ref, z_ref, gates_ref, hb_ref,
                    *, rows, chunk):
    norm, proj_u, proj_z, gate_rows = _stage1_pieces(
        rows, chunk, hb_ref, ng_ref, wa_ref, wb_ref, b_ref, wgt_ref, bgt_ref)
    norm(x_ref[...])
    for j in range(0, QK_WIDTH, PROJ_BLOCK):
        proj_u(j, u_ref)
    for j in range(0, Z_WIDTH, PROJ_BLOCK):
        proj_z(j, z_ref)
    gate_rows(gates_ref)


def _merge_kernel(gmix_ref, gatt_ref, z_ref, x_ref, wbm_ref, wba_ref, wout_ref, y_ref, mix_ref):
    for j in range(0, D_MODEL, PROJ_BLOCK):
        _merge_block(j, z_ref, gmix_ref, gatt_ref, mix_ref, wbm_ref, wba_ref)
    for j in range(0, D_MODEL, PROJ_BLOCK):
        _out_block(j, x_ref, mix_ref, wout_ref, y_ref)


def _const_spec(shape):
    zeros = (0,) * len(shape)
    return pl.BlockSpec(shape, lambda i: zeros, pipeline_mode=pl.Buffered(1))


def _full_spec(shape):
    zeros = (0,) * len(shape)
    return pl.BlockSpec(shape, lambda i: zeros)


def _per_stream_spec(shape, n_tiles=1):
    nd = len(shape)
    return pl.BlockSpec((1,) + tuple(shape[1:]), lambda i: (i // n_tiles,) + (0,) * (nd - 1))


def _state_out_shapes(n_streams, keep):
    f32 = jnp.float32
    return (
        jax.ShapeDtypeStruct((n_streams, M_HEADS, M_HEAD_DIM, M_HEAD_DIM), f32),
        jax.ShapeDtypeStruct((n_streams, M_HEADS, LANES), f32),
        jax.ShapeDtypeStruct((n_streams, M_HEADS, LANES), f32),
        jax.ShapeDtypeStruct((n_streams, SUBLANES, QK_WIDTH), f32),
        jax.ShapeDtypeStruct((n_streams, keep, A_WIDTH), f32),
        jax.ShapeDtypeStruct((n_streams, keep, A_WIDTH), f32),
    )


def _compiler_params():
    return pltpu.CompilerParams(dimension_semantics=("arbitrary",), vmem_limit_bytes=VMEM_LIMIT_BYTES)


def _fresh_layer_call(x, stage1_params, mixer_params, merge_params, *, tile, chunk):
    n_streams, seq, _ = x.shape
    n_tiles = seq // tile
    n_steps = n_streams * n_tiles
    assert n_tiles > 1 and WINDOW % tile == 0 and seq % WINDOW == 0

    def tile_spec(step_of):
        return pl.BlockSpec((1, tile, D_MODEL), lambda i: (step_of(i) // n_tiles, step_of(i) % n_tiles, 0))

    params = tuple(stage1_params) + tuple(mixer_params) + tuple(merge_params)
    in_specs = [tile_spec(lambda i: jnp.minimum(i + 1, n_steps - 1)),
                pl.BlockSpec((1, tile, D_MODEL), lambda i: (0, 0, 0), pipeline_mode=pl.Buffered(1))]
    in_specs += [_const_spec(a.shape) for a in params]
    out_shape = (jax.ShapeDtypeStruct(x.shape, jnp.float32),) + _state_out_shapes(n_streams, WINDOW)
    out_specs = (tile_spec(lambda i: i),) + tuple(_per_stream_spec(s.shape, n_tiles) for s in out_shape[1:])
    f32, bf16 = jnp.float32, jnp.bfloat16
    scratch = [
        pltpu.VMEM((tile, D_MODEL), bf16),
        pltpu.VMEM((tile + SUBLANES, QK_WIDTH), f32),
        pltpu.VMEM((2, tile, Z_WIDTH), f32),
        pltpu.VMEM((2, tile, M_WIDTH), f32),
        pltpu.VMEM((2, tile, M_WIDTH), f32),
        pltpu.VMEM((2, 2 * M_HEADS, tile), f32),
        pltpu.VMEM((2, tile, D_MODEL), f32),
        pltpu.VMEM((WINDOW + tile, A_WIDTH), bf16),
        pltpu.VMEM((WINDOW + tile, A_WIDTH), bf16),
        pltpu.VMEM((tile, M_WIDTH), bf16),
        pltpu.VMEM((tile, A_WIDTH), bf16),
        pltpu.VMEM((tile, D_MODEL), bf16),
        pltpu.VMEM((M_HEADS, M_HEAD_DIM, M_HEAD_DIM), f32),
        pltpu.VMEM((SUBLANES, LANES), f32),
    ]
    kern = functools.partial(_layer_kernel, tile=tile, chunk=chunk, n_tiles=n_tiles, n_steps=n_steps,
                             pipelined=True)
    return pl.pallas_call(
        kern, grid=(n_steps,), in_specs=in_specs, out_specs=out_specs, out_shape=out_shape,
        scratch_shapes=scratch, compiler_params=_compiler_params(), name="layer_fresh",
    )(x, x, *params)


def _running_layer_call(x, state, stage1_params, mixer_params, merge_params):
    n_streams, seq, _ = x.shape
    rows = n_streams * seq
    f32, bf16 = jnp.float32, jnp.bfloat16
    x_rows = x.reshape(rows, D_MODEL)

    u, z, gate_rows = pl.pallas_call(
        functools.partial(_project_kernel, rows=rows, chunk=seq),
        grid=(1,),
        in_specs=[_const_spec(x_rows.shape)] + [_const_spec(a.shape) for a in stage1_params],
        out_specs=(_full_spec((rows, QK_WIDTH)), _full_spec((rows, Z_WIDTH)), _full_spec((2 * M_HEADS, rows))),
        out_shape=(jax.ShapeDtypeStruct((rows, QK_WIDTH), f32), jax.ShapeDtypeStruct((rows, Z_WIDTH), f32),
                   jax.ShapeDtypeStruct((2 * M_HEADS, rows), f32)),
        scratch_shapes=[pltpu.VMEM((rows, D_MODEL), bf16)],
        compiler_params=_compiler_params(), name="running_project",
    )(x_rows, *stage1_params)

    per_stream_in = (u.reshape(n_streams, seq, QK_WIDTH), z.reshape(n_streams, seq, Z_WIDTH),
                     gate_rows.reshape(2 * M_HEADS, n_streams, seq).transpose(1, 0, 2)) + tuple(state)
    out_shape = (jax.ShapeDtypeStruct((n_streams, seq, M_WIDTH), bf16),
                 jax.ShapeDtypeStruct((n_streams, seq, A_WIDTH), bf16)) + _state_out_shapes(n_streams, seq)
    scratch = [
        pltpu.VMEM((seq + SUBLANES, QK_WIDTH), f32),
        pltpu.VMEM((seq, M_WIDTH), f32),
        pltpu.VMEM((seq, M_WIDTH), f32),
        pltpu.VMEM((WINDOW + seq, A_WIDTH), bf16),
        pltpu.VMEM((WINDOW + seq, A_WIDTH), bf16),
        pltpu.VMEM((M_HEADS, M_HEAD_DIM, M_HEAD_DIM), f32),
        pltpu.VMEM((SUBLANES, LANES), f32),
    ]
    outs = pl.pallas_call(
        functools.partial(_layer_kernel, tile=seq, chunk=seq, n_tiles=1, n_steps=n_streams, pipelined=False),
        grid=(n_streams,),
        in_specs=[_per_stream_spec(a.shape) for a in per_stream_in] + [_const_spec(a.shape) for a in mixer_params],
        out_specs=tuple(_per_stream_spec(s.shape) for s in out_shape),
        out_shape=out_shape, scratch_shapes=scratch,
        compiler_params=_compiler_params(), name="running_mixers",
    )(*per_stream_in, *mixer_params)
    gmix, gatt = outs[0].reshape(rows, M_WIDTH), outs[1].reshape(rows, A_WIDTH)

    merge_in = (gmix, gatt, z, x_rows) + tuple(merge_params)
    y = pl.pallas_call(
        _merge_kernel, grid=(1,),
        in_specs=[_const_spec(a.shape) for a in merge_in],
        out_specs=_full_spec((rows, D_MODEL)),
        out_shape=jax.ShapeDtypeStruct((rows, D_MODEL), f32),
        scratch_shapes=[pltpu.VMEM((rows, D_MODEL), bf16)],
        compiler_params=_compiler_params(), name="running_merge",
    )(*merge_in)
    return (y.reshape(x.shape),) + tuple(outs[2:])


def _pair_bias(rel_bias, chunk):
    assert chunk - 1 <= REL_CLIP <= WINDOW and chunk + REL_CLIP <= 2 * REL_CLIP + 1
    width = WINDOW + chunk - BIAS_FAR
    n_ext = width + chunk - 1
    rev = rel_bias[:, ::-1].astype(jnp.float32)
    ext = jnp.concatenate([jnp.broadcast_to(rev[:, :1], (A_HEADS, chunk - 1)), rev[:, :width]], axis=1)
    ext = ext - rev[:, :1]
    flat = jnp.tile(ext, (1, chunk))[:, chunk - 1:chunk - 1 + chunk * (n_ext - 1)]
    bias = flat.reshape(A_HEADS, chunk, n_ext - 1)[:, :, :width]
    return bias.reshape(A_PAIRS, 2 * chunk, width)


def _layer_params(norm_g, w_in, b_in, conv_w, conv_b, m_head_g, q_norm_g, k_norm_g, w_bm, w_ba, w_out):
    gate_lo = W_A_WIDTH
    gate_hi = gate_lo + 2 * M_HEADS
    w_a = w_in[:, :gate_lo].astype(jnp.bfloat16)
    w_b = w_in[:, gate_hi:].astype(jnp.bfloat16)
    b_main = jnp.concatenate([b_in[:gate_lo], b_in[gate_hi:]])[None, :]
    w_gate_t = w_in[:, gate_lo:gate_hi].T.astype(jnp.bfloat16)
    stage1 = (norm_g[None, :], w_a, w_b, b_main, w_gate_t, b_in[gate_lo:gate_hi, None])
    mixers = (conv_w, conv_b[None, :], m_head_g, jnp.tile(q_norm_g, 2)[None, :], jnp.tile(k_norm_g, 2)[None, :])
    merge = (w_bm.astype(jnp.bfloat16), w_ba.astype(jnp.bfloat16), w_out.astype(jnp.bfloat16))
    return stage1, mixers, merge


PROMPT_TILE = 256


def kernel(x_prompt, x_sample, state_mlstm_C, state_mlstm_n, state_mlstm_m, state_mlstm_conv,
           cache_attn_k, cache_attn_v, norm_g, w_in, b_in, conv_w, conv_b, m_head_g,
           q_norm_g, k_norm_g, rel_bias, w_bm, w_ba, w_out):
    assert tuple(int(w) for w in IN_WIDTHS[:5]) == (M_WIDTH,) * 5 and sum(IN_WIDTHS) == w_in.shape[-1]
    depth = w_in.shape[0]
    xp, xs = x_prompt, x_sample
    n_p = xp.shape[0]
    n_s, t_s, _ = xs.shape

    def unpack(c, n, m, conv, k, v):
        n_streams = c.shape[0]
        return (c, n, m[:, :, 0], conv[:, SUBLANES - (CONV_WIDTH - 1):],
                k.reshape(n_streams, -1, A_HEADS, A_HEAD_DIM), v.reshape(n_streams, -1, A_HEADS, A_HEAD_DIM))

    outs_p, outs_s = [], []
    for l in range(depth):
        stage1, mixers, merge = _layer_params(norm_g[l], w_in[l], b_in[l], conv_w[l], conv_b[l], m_head_g[l],
                                              q_norm_g[l], k_norm_g[l], w_bm[l], w_ba[l], w_out[l])
        xp, *state_p = _fresh_layer_call(xp, stage1, mixers + (_pair_bias(rel_bias[l], CHUNK),), merge,
                                         tile=PROMPT_TILE, chunk=CHUNK)
        outs_p.append(unpack(*state_p))
        state = (state_mlstm_C[l], state_mlstm_n[l],
                 jnp.broadcast_to(state_mlstm_m[l][:, :, None], (n_s, M_HEADS, LANES)),
                 jnp.pad(state_mlstm_conv[l], ((0, 0), (SUBLANES - (CONV_WIDTH - 1), 0), (0, 0))),
                 cache_attn_k[l].reshape(n_s, -1, A_WIDTH), cache_attn_v[l].reshape(n_s, -1, A_WIDTH))
        xs, *state_s = _running_layer_call(xs, state, stage1, mixers + (_pair_bias(rel_bias[l], t_s),), merge)
        outs_s.append(unpack(*state_s))
    stack = lambda outs, i: jnp.stack([o[i] for o in outs])
    return (xp, xs) + tuple(stack(outs_p, i) for i in range(6)) + tuple(stack(outs_s, i) for i in range(6))
```

```python
import functools

import jax
import jax.numpy as jnp
import numpy as np
from jax import lax
from jax.experimental import pallas as pl
from jax.experimental.pallas import tpu as pltpu

D_MODEL = 1024
CHUNK = 64
M_HEADS = 4
M_HEAD_DIM = 128
M_WIDTH = M_HEADS * M_HEAD_DIM
CONV_WIDTH = 4
A_HEADS = 8
A_HEAD_DIM = 64
A_WIDTH = A_HEADS * A_HEAD_DIM
A_PAIRS = A_HEADS // 2
WINDOW = 8 * CHUNK
REL_CLIP = 128
BIAS_FAR = WINDOW - REL_CLIP
EPS = 1e-6
IN_WIDTHS = (M_WIDTH, M_WIDTH, M_WIDTH, M_WIDTH, M_WIDTH, M_HEADS, M_HEADS,
             A_WIDTH, A_WIDTH, A_WIDTH, A_WIDTH, D_MODEL, D_MODEL)

LANES = 128
SUBLANES = 8
VMEM_LIMIT_BYTES = 60 * 1024 * 1024

Z_MV = 0
Z_MO = Z_MV + M_WIDTH
Z_MZ = Z_MO + M_WIDTH
Z_AQ = Z_MZ + M_WIDTH
Z_AK = Z_AQ + A_WIDTH
Z_AV = Z_AK + A_WIDTH
Z_AZ = Z_AV + A_WIDTH
Z_GM = Z_AZ + A_WIDTH
Z_GA = Z_GM + D_MODEL
Z_WIDTH = Z_GA + D_MODEL
QK_WIDTH = 2 * M_WIDTH
W_A_WIDTH = QK_WIDTH + Z_AQ
PROJ_BLOCK = 512
CONV_GROUPS = 4

_NT = (((1,), (1,)), ((), ()))


def _sigmoid(x):
    return 1.0 / (1.0 + jnp.exp(-x))


def _silu(x):
    return x * _sigmoid(x)


def _log_sigmoid(x):
    return jnp.minimum(x, 0.0) - jnp.log1p(jnp.exp(-jnp.abs(x)))


def _bf16(x):
    return x.astype(jnp.bfloat16)


def _dot(a, b):
    return jnp.dot(a, b, preferred_element_type=jnp.float32)


def _split3(x):
    hi = _bf16(x)
    r = x - hi.astype(jnp.float32)
    mid = _bf16(r)
    lo = _bf16(r - mid.astype(jnp.float32))
    return hi, mid, lo


def _stage1_pieces(rows, chunk, hb_ref, ng_ref, wa_ref, wb_ref, b_ref, wgt_ref, bgt_ref):
    def norm(x):
        inv = lax.rsqrt(jnp.mean(x * x, axis=-1, keepdims=True) + EPS)
        hb_ref[...] = _bf16(x * inv * ng_ref[...])

    def proj_u(j, u_dst):
        u_dst[:, j:j + PROJ_BLOCK] = _dot(hb_ref[...], wa_ref[:, j:j + PROJ_BLOCK]) + b_ref[:, j:j + PROJ_BLOCK]

    def proj_z(j, z_dst):
        if j < Z_AQ:
            w = wa_ref[:, QK_WIDTH + j:QK_WIDTH + j + PROJ_BLOCK]
        else:
            w = wb_ref[:, j - Z_AQ:j - Z_AQ + PROJ_BLOCK]
        z_dst[:, j:j + PROJ_BLOCK] = _dot(hb_ref[...], w) + b_ref[:, QK_WIDTH + j:QK_WIDTH + j + PROJ_BLOCK]

    def gate_pre():
        g_rows = lax.dot_general(wgt_ref[...], hb_ref[...], _NT,
                                 preferred_element_type=jnp.float32) + bgt_ref[...]
        return g_rows, _log_sigmoid(g_rows)

    def gate_rows(gates_dst, g_rows, lf_all):
        tr = lax.broadcasted_iota(jnp.int32, (rows, rows), 0)
        tc = lax.broadcasted_iota(jnp.int32, (rows, rows), 1)
        same_chunk = jnp.bitwise_xor(tr, tc) < chunk
        cum_mat = jnp.where((tr <= tc) & same_chunk, 1.0, 0.0).astype(jnp.bfloat16)
        hi, mid, lo3 = _split3(lf_all)
        b_all = _dot(hi, cum_mat) + _dot(mid, cum_mat) + _dot(lo3, cum_mat)
        gates_dst[...] = jnp.concatenate(
            [g_rows[0:M_HEADS, :] - b_all[M_HEADS:2 * M_HEADS, :], lf_all[M_HEADS:2 * M_HEADS, :]], axis=0)

    return norm, proj_u, proj_z, gate_pre, gate_rows


def _merge_block(j, z, gmix, gatt, mix_ref, wbm_ref, wba_ref):
    cs = slice(j, j + PROJ_BLOCK)
    u_m = _dot(gmix[...], wbm_ref[:, cs])
    u_a = _dot(gatt[...], wba_ref[:, cs])
    mix_ref[:, cs] = _bf16(_sigmoid(z[:, Z_GM + j:Z_GM + j + PROJ_BLOCK]) * u_m
                           + _sigmoid(z[:, Z_GA + j:Z_GA + j + PROJ_BLOCK]) * u_a)


def _out_block(j, xres, mix_ref, wout_ref, y_dst):
    cs = slice(j, j + PROJ_BLOCK)
    y_dst[:, cs] = xres[:, cs] + _dot(mix_ref[...], wout_ref[:, cs])


def _layer_kernel(*refs, tile, chunk, n_tiles, n_steps, pipelined):
    has_state = not pipelined
    n_chunks = tile // chunk
    band = WINDOW + chunk
    it = iter(refs)
    if pipelined:
        x_ref, x0_ref = next(it), next(it)
        ng_ref, wa_ref, wb_ref, b_ref, wgt_ref, bgt_ref = (next(it) for _ in range(6))
    else:
        u_in_ref, z_in_ref, gates_in_ref = (next(it) for _ in range(3))
        c0_ref, n0_ref, m0_ref, conv0_ref, k0_ref, v0_ref = (next(it) for _ in range(6))
    cw_ref, cb_ref, mg_ref, qg_ref, kg_ref, bias_ref = (next(it) for _ in range(6))
    if pipelined:
        wbm_ref, wba_ref, wout_ref = (next(it) for _ in range(3))
        y_ref = next(it)
    else:
        gmix_out_ref, gatt_out_ref = next(it), next(it)
    c_ref, n_out_ref, m_out_ref, conv_out_ref, k_out_ref, v_out_ref = (next(it) for _ in range(6))
    if pipelined:
        hb_ref, u_ref, z2_ref, qs2_ref, ks2_ref, gates2_ref, xres2_ref = (next(it) for _ in range(7))
        kh_ref, vh_ref, gmix_ref, gatt_ref, mix_ref, nd_ref, m_ref = (next(it) for _ in range(7))
    else:
        u_ref, qs_ref, ks_ref, kh_ref, vh_ref, nd_ref, m_ref = (next(it) for _ in range(7))

    i = pl.program_id(0)
    t2 = i % n_tiles
    if pipelined:
        t1 = jnp.minimum(i + 1, n_steps - 1) % n_tiles
        slot1, slot2 = (i + 1) % 2, i % 2
        z, qs, ks = z2_ref.at[slot2], qs2_ref.at[slot2], ks2_ref.at[slot2]
        gates, xres = gates2_ref.at[slot2], xres2_ref.at[slot2]
        gmix, gatt = gmix_ref, gatt_ref
    else:
        z, qs, ks, gates = z_in_ref.at[0], qs_ref, ks_ref, gates_in_ref.at[0]
        gmix, gatt = gmix_out_ref.at[0], gatt_out_ref.at[0]

    neg_inf = jnp.float32(-jnp.inf)
    lo = lax.broadcasted_iota(jnp.int32, (1, LANES), 1) < A_HEAD_DIM

    def chunk_rows(r):
        return jnp.stack([r[:, c * chunk:(c + 1) * chunk] for c in range(n_chunks)], axis=0)

    def chunked(a):
        return a.reshape(n_chunks, chunk, a.shape[-1])

    def wide(a):
        return jnp.concatenate([a, a], axis=-1)

    def conv(group, qs_dst, ks_dst):
        per = QK_WIDTH // LANES // CONV_GROUPS
        for j in range(group * per * LANES, (group + 1) * per * LANES, LANES):
            cs = slice(j, j + LANES)
            acc = cb_ref[:, cs] + cw_ref[CONV_WIDTH - 1:CONV_WIDTH, cs] * u_ref[SUBLANES:SUBLANES + tile, cs]
            for d in range(1, CONV_WIDTH):
                acc = acc + (cw_ref[CONV_WIDTH - 1 - d:CONV_WIDTH - d, cs]
                             * u_ref[SUBLANES - d:SUBLANES - d + tile, cs])
            act = _silu(acc)
            if j < M_WIDTH:
                qs_dst[:, cs] = act
            else:
                ks_dst[:, j - M_WIDTH:j - M_WIDTH + LANES] = act * (M_HEAD_DIM ** -0.5)

    def conv_tail(t_tile, live):
        @pl.when((t_tile == n_tiles - 1) & live)
        def _conv_out():
            conv_out_ref[0] = u_ref[tile:tile + SUBLANES, :]

        if n_tiles > 1:
            u_ref[0:SUBLANES, :] = u_ref[tile:tile + SUBLANES, :]

    row = lax.broadcasted_iota(jnp.int32, (1, chunk, chunk), 1)
    col = lax.broadcasted_iota(jnp.int32, (1, chunk, chunk), 2)
    lower = row >= col
    ones_v = jnp.ones((n_chunks, chunk, M_HEAD_DIM), jnp.bfloat16)

    def mlstm_intra(h):
        cs = slice(h * M_HEAD_DIM, (h + 1) * M_HEAD_DIM)
        q = chunked(qs[:, cs])
        k = chunked(ks[:, cs])
        qb, kb = _bf16(q), _bf16(k)
        v_aug = jnp.concatenate(
            [_bf16(chunked(z[:, Z_MV + h * M_HEAD_DIM:Z_MV + (h + 1) * M_HEAD_DIM])), ones_v], axis=2)
        a_row = chunk_rows(gates[h:h + 1, :])
        lf_row = chunk_rows(gates[M_HEADS + h:M_HEADS + h + 1, :])
        b_col = jnp.sum(jnp.where(lower, lf_row, 0.0), axis=2, keepdims=True)
        m_loc = jnp.max(jnp.where(lower, a_row, neg_inf), axis=2, keepdims=True)
        decay = jnp.where(lower, jnp.exp(a_row - m_loc), 0.0)
        s = jnp.einsum("ctd,csd->cts", qb, kb, preferred_element_type=jnp.float32) * decay
        kw_t = _bf16(jnp.swapaxes(k, 1, 2) * decay[:, chunk - 1:chunk, :])
        return dict(h=h, qb=qb, v_aug=v_aug, b_col=b_col, m_loc=m_loc, s=_bf16(s), kw_t=kw_t)

    def mlstm_products(st):
        st["pv"] = jnp.einsum("cts,csd->ctd", st.pop("s"), st["v_aug"], preferred_element_type=jnp.float32)
        st["kv"] = jnp.einsum("cks,csd->ckd", st.pop("kw_t"), st.pop("v_aug"), preferred_element_type=jnp.float32)

    def mlstm_scan(st):
        h, m_loc, b_col, kv = st["h"], st["m_loc"], st["b_col"], st.pop("kv")
        m_loc_end = m_loc[:, chunk - 1:chunk, :]
        b_end = b_col[:, chunk - 1:chunk, :]
        cn_run = jnp.concatenate([c_ref[0, h], nd_ref[h]], axis=1)
        m_run = m_ref[h:h + 1, :]
        cn_start, m_start = [], []
        for c in range(n_chunks):
            cn_start.append(_bf16(cn_run))
            m_start.append(m_run)
            m_end = jnp.maximum(m_loc_end[c], m_run)
            cn_run = wide(jnp.exp(m_run - m_end)) * cn_run + wide(jnp.exp(m_loc_end[c] - m_end)) * kv[c]
            m_run = b_end[c] + m_end
        c_ref[0, h] = cn_run[:, 0:M_HEAD_DIM]
        nd_ref[h] = cn_run[:, M_HEAD_DIM:2 * M_HEAD_DIM]
        m_ref[h:h + 1, :] = m_run
        st["m0"] = jnp.stack(m_start, axis=0)
        st["qc"] = jnp.einsum("ctk,ckv->ctv", st.pop("qb"), jnp.stack(cn_start, axis=0),
                              preferred_element_type=jnp.float32)

    def mlstm_out(st):
        h, m_loc, b_col = st["h"], st["m_loc"], st["b_col"]
        cs = slice(h * M_HEAD_DIM, (h + 1) * M_HEAD_DIM)
        m_max = jnp.maximum(m_loc, st["m0"])
        w_intra = jnp.exp(m_loc - m_max)
        w_state = jnp.exp(st["m0"] - m_max)
        both = wide(w_intra) * st["pv"] + wide(w_state) * st["qc"]
        num, den = both[:, :, 0:M_HEAD_DIM], both[:, :, M_HEAD_DIM:2 * M_HEAD_DIM]
        hh = num / jnp.maximum(jnp.abs(den), jnp.exp(-(b_col + m_max)))
        hn = hh * lax.rsqrt(jnp.mean(hh * hh, axis=2, keepdims=True) + EPS) * mg_ref[h:h + 1, :]
        hn = hn.reshape(tile, M_HEAD_DIM)
        hm = _sigmoid(z[:, Z_MO + h * M_HEAD_DIM:Z_MO + (h + 1) * M_HEAD_DIM]) * hn
        gmix[:, cs] = _bf16(hm * _silu(z[:, Z_MZ + h * M_HEAD_DIM:Z_MZ + (h + 1) * M_HEAD_DIM]))

    def head_norm(a, g):
        sq = a * a
        s_lo = jnp.sum(jnp.where(lo, sq, 0.0), axis=1, keepdims=True)
        s_hi = jnp.sum(jnp.where(lo, 0.0, sq), axis=1, keepdims=True)
        ms = jnp.where(lo, s_lo, s_hi) * (1.0 / A_HEAD_DIM)
        return a * lax.rsqrt(ms + EPS) * g

    def band_write(p):
        cs = slice(p * LANES, (p + 1) * LANES)
        kn = head_norm(z[:, Z_AK + p * LANES:Z_AK + (p + 1) * LANES], kg_ref[...])
        vv = z[:, Z_AV + p * LANES:Z_AV + (p + 1) * LANES]
        kh_ref[WINDOW:WINDOW + tile, cs] = _bf16(kn)
        vh_ref[WINDOW:WINDOW + tile, cs] = _bf16(vv)
        if n_tiles > 1:
            ring = pl.multiple_of((t2 * tile) % WINDOW, tile)
            k_out_ref[0, pl.ds(ring, tile), cs] = kn
            v_out_ref[0, pl.ds(ring, tile), cs] = vv
        else:
            k_out_ref[0, :, cs] = kn
            v_out_ref[0, :, cs] = vv

    bands = [slice(c * chunk, c * chunk + band) for c in range(n_chunks)]
    ones_band = jnp.ones((band, LANES), jnp.bfloat16)

    def attention_scores(p):
        cs = slice(p * LANES, (p + 1) * LANES)
        qn = head_norm(z[:, Z_AQ + p * LANES:Z_AQ + (p + 1) * LANES], qg_ref[...])
        qn = chunked(qn * (A_HEAD_DIM ** -0.5))
        q2 = _bf16(jnp.concatenate([jnp.where(lo, qn, 0.0), jnp.where(lo, 0.0, qn)], axis=1))
        s = jnp.stack([lax.dot_general(q2[c], kh_ref[bands[c], cs], _NT, preferred_element_type=jnp.float32)
                       for c in range(n_chunks)], axis=0)
        s = jnp.concatenate([s[:, :, 0:BIAS_FAR], s[:, :, BIAS_FAR:band] + bias_ref[p]], axis=2)
        if not has_state:
            kcol = lax.broadcasted_iota(jnp.int32, (n_chunks, 1, band), 2)
            first_valid = WINDOW - t2 * tile - chunk * lax.broadcasted_iota(jnp.int32, (n_chunks, 1, band), 0)
            s = jnp.where(kcol >= first_valid, s, neg_inf)
        return p, _bf16(jnp.exp(s - jnp.max(s, axis=2, keepdims=True)))

    def attention_values(p, eb):
        cs = slice(p * LANES, (p + 1) * LANES)
        o2 = jnp.stack([_dot(eb[c], jnp.concatenate([vh_ref[bands[c], cs], ones_band], axis=1))
                        for c in range(n_chunks)], axis=0)
        o2 = o2[:, :, 0:LANES] / o2[:, :, LANES:2 * LANES]
        o = jnp.where(lo, o2[:, 0:chunk], o2[:, chunk:2 * chunk]).reshape(tile, LANES)
        gatt[:, cs] = _bf16(o * _silu(z[:, Z_AZ + p * LANES:Z_AZ + (p + 1) * LANES]))

    def attention_all(fillers):
        fillers = list(fillers)
        pending = None
        for p in range(A_PAIRS):
            nxt = attention_scores(p)
            if pending is not None:
                attention_values(*pending)
            pending = nxt
            if fillers:
                fillers.pop(0)()
        attention_values(*pending)
        for f in fillers:
            f()

    def band_slide():
        for r in range(0, WINDOW, LANES):
            kh_ref[r:r + LANES, :] = kh_ref[tile + r:tile + r + LANES, :]
            vh_ref[r:r + LANES, :] = vh_ref[tile + r:tile + r + LANES, :]

    if pipelined:
        norm, proj_u, proj_z, gate_pre, gate_rows = _stage1_pieces(
            tile, chunk, hb_ref, ng_ref, wa_ref, wb_ref, b_ref, wgt_ref, bgt_ref)
        u_new = u_ref.at[SUBLANES:SUBLANES + tile]

        @pl.when(i == 0)
        def _prologue():
            u_ref[0:SUBLANES, :] = jnp.zeros((SUBLANES, QK_WIDTH), jnp.float32)
            x0 = x0_ref[0]
            xres2_ref[0] = x0
            norm(x0)
            for j in range(0, QK_WIDTH, PROJ_BLOCK):
                proj_u(j, u_new)
            for j in range(0, Z_WIDTH, PROJ_BLOCK):
                proj_z(j, z2_ref.at[0])
            gate_rows(gates2_ref.at[0], *gate_pre())
            for g in range(CONV_GROUPS):
                conv(g, qs2_ref.at[0], ks2_ref.at[0])
            conv_tail(0, True)

    @pl.when(t2 == 0)
    def _init_stream():
        if has_state:
            c_ref[0] = c0_ref[0]
            for h in range(M_HEADS):
                nd_ref[h] = jnp.broadcast_to(n0_ref[0, h:h + 1, :], (M_HEAD_DIM, M_HEAD_DIM)).T
            m_ref[0:M_HEADS, :] = m0_ref[0]
            kh_ref[0:WINDOW, :] = _bf16(k0_ref[0])
            vh_ref[0:WINDOW, :] = _bf16(v0_ref[0])
        else:
            c_ref[...] = jnp.zeros_like(c_ref)
            nd_ref[...] = jnp.zeros_like(nd_ref)
            m_ref[...] = jnp.zeros_like(m_ref)
            kh_ref[0:WINDOW, :] = jnp.zeros((WINDOW, A_WIDTH), jnp.bfloat16)
            vh_ref[0:WINDOW, :] = jnp.zeros((WINDOW, A_WIDTH), jnp.bfloat16)

    if pipelined:
        @pl.when(t1 == 0)
        def _init_conv_tail():
            u_ref[0:SUBLANES, :] = jnp.zeros((SUBLANES, QK_WIDTH), jnp.float32)

        z1, qs1, ks1 = z2_ref.at[slot1], qs2_ref.at[slot1], ks2_ref.at[slot1]
        blocks = iter(range(0, Z_WIDTH, PROJ_BLOCK))
        x1 = x_ref[0]
        xres2_ref[slot1] = x1
        norm(x1)
        heads = [mlstm_intra(h) for h in range(M_HEADS)]
        proj_u(0, u_new)
        proj_u(PROJ_BLOCK, u_new)
        gate_pre_acts = gate_pre()
        for st in heads:
            mlstm_products(st)
        proj_z(next(blocks), z1)
        proj_z(next(blocks), z1)
        for st in heads:
            mlstm_scan(st)
        proj_z(next(blocks), z1)
        gate_rows(gates2_ref.at[slot1], *gate_pre_acts)
        for p in range(A_PAIRS):
            band_write(p)
        for st in heads:
            mlstm_out(st)
            proj_z(next(blocks), z1)
        attention_all([functools.partial(proj_z, j, z1) for j in blocks])
        band_slide()
        for g, j in enumerate(range(0, D_MODEL, PROJ_BLOCK)):
            _merge_block(j, z, gmix, gatt, mix_ref, wbm_ref, wba_ref)
            conv(g, qs1, ks1)
        for g, j in enumerate(range(0, D_MODEL, PROJ_BLOCK)):
            _out_block(j, xres, mix_ref, wout_ref, y_ref.at[0])
            conv(D_MODEL // PROJ_BLOCK + g, qs1, ks1)
        conv_tail(t1, i + 1 < n_steps)
    else:
        u_ref[0:SUBLANES, :] = conv0_ref[0]
        u_ref[SUBLANES:SUBLANES + tile, :] = u_in_ref[0]
        for g in range(CONV_GROUPS):
            conv(g, qs, ks)
        conv_tail(t2, True)
        heads = [mlstm_intra(h) for h in range(M_HEADS)]
        for st in heads:
            mlstm_products(st)
        for st in heads:
            mlstm_scan(st)
        for p in range(A_PAIRS):
            band_write(p)
        for st in heads:
            mlstm_out(st)
        attention_all([])

    @pl.when(t2 == n_tiles - 1)
    def _state_out():
        for h in range(M_HEADS):
            n_out_ref[0, h:h + 1, :] = nd_ref[h].T[0:1, :]
        m_out_ref[0] = m_ref[0:M_HEADS, :]


def _project_kernel(x_ref, ng_ref, wa_ref, wb_ref, b_ref, wgt_ref, bgt_ref, u_ref, z_ref, gates_ref, hb_ref,
                    *, rows, chunk):
    norm, proj_u, proj_z, gate_pre, gate_rows = _stage1_pieces(
        rows, chunk, hb_ref, ng_ref, wa_ref, wb_ref, b_ref, wgt_ref, bgt_ref)
    norm(x_ref[...])
    for j in range(0, QK_WIDTH, PROJ_BLOCK):
        proj_u(j, u_ref)
    for j in range(0, Z_WIDTH, PROJ_BLOCK):
        proj_z(j, z_ref)
    gate_rows(gates_ref, *gate_pre())


def _merge_kernel(gmix_ref, gatt_ref, z_ref, x_ref, wbm_ref, wba_ref, wout_ref, y_ref, mix_ref):
    for j in range(0, D_MODEL, PROJ_BLOCK):
        _merge_block(j, z_ref, gmix_ref, gatt_ref, mix_ref, wbm_ref, wba_ref)
    for j in range(0, D_MODEL, PROJ_BLOCK):
        _out_block(j, x_ref, mix_ref, wout_ref, y_ref)


def _const_spec(shape):
    zeros = (0,) * len(shape)
    return pl.BlockSpec(shape, lambda i: zeros, pipeline_mode=pl.Buffered(1))


def _full_spec(shape):
    zeros = (0,) * len(shape)
    return pl.BlockSpec(shape, lambda i: zeros)


def _per_stream_spec(shape, n_tiles=1):
    nd = len(shape)
    return pl.BlockSpec((1,) + tuple(shape[1:]), lambda i: (i // n_tiles,) + (0,) * (nd - 1))


def _state_out_shapes(n_streams, keep):
    f32 = jnp.float32
    return (
        jax.ShapeDtypeStruct((n_streams, M_HEADS, M_HEAD_DIM, M_HEAD_DIM), f32),
        jax.ShapeDtypeStruct((n_streams, M_HEADS, LANES), f32),
        jax.ShapeDtypeStruct((n_streams, M_HEADS, LANES), f32),
        jax.ShapeDtypeStruct((n_streams, SUBLANES, QK_WIDTH), f32),
        jax.ShapeDtypeStruct((n_streams, keep, A_WIDTH), f32),
        jax.ShapeDtypeStruct((n_streams, keep, A_WIDTH), f32),
    )


def _compiler_params():
    return pltpu.CompilerParams(dimension_semantics=("arbitrary",), vmem_limit_bytes=VMEM_LIMIT_BYTES)


def _fresh_layer_call(x, stage1_params, mixer_params, merge_params, *, tile, chunk):
    n_streams, seq, _ = x.shape
    n_tiles = seq // tile
    n_steps = n_streams * n_tiles
    assert n_tiles > 1 and WINDOW % tile == 0 and seq % WINDOW == 0

    def tile_spec(step_of):
        return pl.BlockSpec((1, tile, D_MODEL), lambda i: (step_of(i) // n_tiles, step_of(i) % n_tiles, 0))

    params = tuple(stage1_params) + tuple(mixer_params) + tuple(merge_params)
    in_specs = [tile_spec(lambda i: jnp.minimum(i + 1, n_steps - 1)),
                pl.BlockSpec((1, tile, D_MODEL), lambda i: (0, 0, 0), pipeline_mode=pl.Buffered(1))]
    in_specs += [_const_spec(a.shape) for a in params]
    out_shape = (jax.ShapeDtypeStruct(x.shape, jnp.float32),) + _state_out_shapes(n_streams, WINDOW)
    out_specs = (tile_spec(lambda i: i),) + tuple(_per_stream_spec(s.shape, n_tiles) for s in out_shape[1:])
    f32, bf16 = jnp.float32, jnp.bfloat16
    scratch = [
        pltpu.VMEM((tile, D_MODEL), bf16),
        pltpu.VMEM((tile + SUBLANES, QK_WIDTH), f32),
        pltpu.VMEM((2, tile, Z_WIDTH), f32),
        pltpu.VMEM((2, tile, M_WIDTH), f32),
        pltpu.VMEM((2, tile, M_WIDTH), f32),
        pltpu.VMEM((2, 2 * M_HEADS, tile), f32),
        pltpu.VMEM((2, tile, D_MODEL), f32),
        pltpu.VMEM((WINDOW + tile, A_WIDTH), bf16),
        pltpu.VMEM((WINDOW + tile, A_WIDTH), bf16),
        pltpu.VMEM((tile, M_WIDTH), bf16),
        pltpu.VMEM((tile, A_WIDTH), bf16),
        pltpu.VMEM((tile, D_MODEL), bf16),
        pltpu.VMEM((M_HEADS, M_HEAD_DIM, M_HEAD_DIM), f32),
        pltpu.VMEM((SUBLANES, LANES), f32),
    ]
    kern = functools.partial(_layer_kernel, tile=tile, chunk=chunk, n_tiles=n_tiles, n_steps=n_steps,
                             pipelined=True)
    return pl.pallas_call(
        kern, grid=(n_steps,), in_specs=in_specs, out_specs=out_specs, out_shape=out_shape,
        scratch_shapes=scratch, compiler_params=_compiler_params(), name="layer_fresh",
    )(x, x, *params)


def _running_layer_call(x, state, stage1_params, mixer_params, merge_params):
    n_streams, seq, _ = x.shape
    rows = n_streams * seq
    f32, bf16 = jnp.float32, jnp.bfloat16
    x_rows = x.reshape(rows, D_MODEL)

    u, z, gate_rows = pl.pallas_call(
        functools.partial(_project_kernel, rows=rows, chunk=seq),
        grid=(1,),
        in_specs=[_const_spec(x_rows.shape)] + [_const_spec(a.shape) for a in stage1_params],
        out_specs=(_full_spec((rows, QK_WIDTH)), _full_spec((rows, Z_WIDTH)), _full_spec((2 * M_HEADS, rows))),
        out_shape=(jax.ShapeDtypeStruct((rows, QK_WIDTH), f32), jax.ShapeDtypeStruct((rows, Z_WIDTH), f32),
                   jax.ShapeDtypeStruct((2 * M_HEADS, rows), f32)),
        scratch_shapes=[pltpu.VMEM((rows, D_MODEL), bf16)],
        compiler_params=_compiler_params(), name="running_project",
    )(x_rows, *stage1_params)

    per_stream_in = (u.reshape(n_streams, seq, QK_WIDTH), z.reshape(n_streams, seq, Z_WIDTH),
                     gate_rows.reshape(2 * M_HEADS, n_streams, seq).transpose(1, 0, 2)) + tuple(state)
    out_shape = (jax.ShapeDtypeStruct((n_streams, seq, M_WIDTH), bf16),
                 jax.ShapeDtypeStruct((n_streams, seq, A_WIDTH), bf16)) + _state_out_shapes(n_streams, seq)
    scratch = [
        pltpu.VMEM((seq + SUBLANES, QK_WIDTH), f32),
        pltpu.VMEM((seq, M_WIDTH), f32),
        pltpu.VMEM((seq, M_WIDTH), f32),
        pltpu.VMEM((WINDOW + seq, A_WIDTH), bf16),
        pltpu.VMEM((WINDOW + seq, A_WIDTH), bf16),
        pltpu.VMEM((M_HEADS, M_HEAD_DIM, M_HEAD_DIM), f32),
        pltpu.VMEM((SUBLANES, LANES), f32),
    ]
    outs = pl.pallas_call(
        functools.partial(_layer_kernel, tile=seq, chunk=seq, n_tiles=1, n_steps=n_streams, pipelined=False),
        grid=(n_streams,),
        in_specs=[_per_stream_spec(a.shape) for a in per_stream_in] + [_const_spec(a.shape) for a in mixer_params],
        out_specs=tuple(_per_stream_spec(s.shape) for s in out_shape),
        out_shape=out_shape, scratch_shapes=scratch,
        compiler_params=_compiler_params(), name="running_mixers",
    )(*per_stream_in, *mixer_params)
    gmix, gatt = outs[0].reshape(rows, M_WIDTH), outs[1].reshape(rows, A_WIDTH)

    merge_in = (gmix, gatt, z, x_rows) + tuple(merge_params)
    y = pl.pallas_call(
        _merge_kernel, grid=(1,),
        in_specs=[_const_spec(a.shape) for a in merge_in],
        out_specs=_full_spec((rows, D_MODEL)),
        out_shape=jax.ShapeDtypeStruct((rows, D_MODEL), f32),
        scratch_shapes=[pltpu.VMEM((rows, D_MODEL), bf16)],
        compiler_params=_compiler_params(), name="running_merge",
    )(*merge_in)
    return (y.reshape(x.shape),) + tuple(outs[2:])


def _pair_bias(rel_bias, chunk):
    assert chunk - 1 <= REL_CLIP <= WINDOW and chunk + REL_CLIP <= 2 * REL_CLIP + 1
    width = WINDOW + chunk - BIAS_FAR
    n_ext = width + chunk - 1
    rev = rel_bias[:, ::-1].astype(jnp.float32)
    ext = jnp.concatenate([jnp.broadcast_to(rev[:, :1], (A_HEADS, chunk - 1)), rev[:, :width]], axis=1)
    ext = ext - rev[:, :1]
    flat = jnp.tile(ext, (1, chunk))[:, chunk - 1:chunk - 1 + chunk * (n_ext - 1)]
    bias = flat.reshape(A_HEADS, chunk, n_ext - 1)[:, :, :width]
    return bias.reshape(A_PAIRS, 2 * chunk, width)


def _layer_params(norm_g, w_in, b_in, conv_w, conv_b, m_head_g, q_norm_g, k_norm_g, w_bm, w_ba, w_out):
    gate_lo = W_A_WIDTH
    gate_hi = gate_lo + 2 * M_HEADS
    w_a = w_in[:, :gate_lo].astype(jnp.bfloat16)
    w_b = w_in[:, gate_hi:].astype(jnp.bfloat16)
    b_main = jnp.concatenate([b_in[:gate_lo], b_in[gate_hi:]])[None, :]
    w_gate_t = w_in[:, gate_lo:gate_hi].T.astype(jnp.bfloat16)
    stage1 = (norm_g[None, :], w_a, w_b, b_main, w_gate_t, b_in[gate_lo:gate_hi, None])
    mixers = (conv_w, conv_b[None, :], m_head_g, jnp.tile(q_norm_g, 2)[None, :], jnp.tile(k_norm_g, 2)[None, :])
    merge = (w_bm.astype(jnp.bfloat16), w_ba.astype(jnp.bfloat16), w_out.astype(jnp.bfloat16))
    return stage1, mixers, merge


PROMPT_TILE = 256


def kernel(x_prompt, x_sample, state_mlstm_C, state_mlstm_n, state_mlstm_m, state_mlstm_conv,
           cache_attn_k, cache_attn_v, norm_g, w_in, b_in, conv_w, conv_b, m_head_g,
           q_norm_g, k_norm_g, rel_bias, w_bm, w_ba, w_out):
    assert tuple(int(w) for w in IN_WIDTHS[:5]) == (M_WIDTH,) * 5 and sum(IN_WIDTHS) == w_in.shape[-1]
    depth = w_in.shape[0]
    xp, xs = x_prompt, x_sample
    n_p = xp.shape[0]
    n_s, t_s, _ = xs.shape

    def unpack(c, n, m, conv, k, v):
        n_streams = c.shape[0]
        return (c, n, m[:, :, 0], conv[:, SUBLANES - (CONV_WIDTH - 1):],
                k.reshape(n_streams, -1, A_HEADS, A_HEAD_DIM), v.reshape(n_streams, -1, A_HEADS, A_HEAD_DIM))

    outs_p, outs_s = [], []
    for l in range(depth):
        stage1, mixers, merge = _layer_params(norm_g[l], w_in[l], b_in[l], conv_w[l], conv_b[l], m_head_g[l],
                                              q_norm_g[l], k_norm_g[l], w_bm[l], w_ba[l], w_out[l])
        xp, *state_p = _fresh_layer_call(xp, stage1, mixers + (_pair_bias(rel_bias[l], CHUNK),), merge,
                                         tile=PROMPT_TILE, chunk=CHUNK)
        outs_p.append(unpack(*state_p))
        state = (state_mlstm_C[l], state_mlstm_n[l],
                 jnp.broadcast_to(state_mlstm_m[l][:, :, None], (n_s, M_HEADS, LANES)),
                 jnp.pad(state_mlstm_conv[l], ((0, 0), (SUBLANES - (CONV_WIDTH - 1), 0), (0, 0))),
                 cache_attn_k[l].reshape(n_s, -1, A_WIDTH), cache_attn_v[l].reshape(n_s, -1, A_WIDTH))
        xs, *state_s = _running_layer_call(xs, state, stage1, mixers + (_pair_bias(rel_bias[l], t_s),), merge)
        outs_s.append(unpack(*state_s))
    stack = lambda outs, i: jnp.stack([o[i] for o in outs])
    return (xp, xs) + tuple(stack(outs_p, i) for i in range(6)) + tuple(stack(outs_s, i) for i in range(6))
```

```python
import functools

import jax
import jax.numpy as jnp
import numpy as np
from jax import lax
from jax.experimental import pallas as pl
from jax.experimental.pallas import tpu as pltpu

D_MODEL = 1024
CHUNK = 64
M_HEADS = 4
M_HEAD_DIM = 128
M_WIDTH = M_HEADS * M_HEAD_DIM
CONV_WIDTH = 4
A_HEADS = 8
A_HEAD_DIM = 64
A_WIDTH = A_HEADS * A_HEAD_DIM
A_PAIRS = A_HEADS // 2
WINDOW = 8 * CHUNK
REL_CLIP = 128
BIAS_FAR = WINDOW - REL_CLIP
EPS = 1e-6
IN_WIDTHS = (M_WIDTH, M_WIDTH, M_WIDTH, M_WIDTH, M_WIDTH, M_HEADS, M_HEADS,
             A_WIDTH, A_WIDTH, A_WIDTH, A_WIDTH, D_MODEL, D_MODEL)

LANES = 128
SUBLANES = 8
VMEM_LIMIT_BYTES = 60 * 1024 * 1024

Z_MV = 0
Z_MO = Z_MV + M_WIDTH
Z_MZ = Z_MO + M_WIDTH
Z_AQ = Z_MZ + M_WIDTH
Z_AK = Z_AQ + A_WIDTH
Z_AV = Z_AK + A_WIDTH
Z_AZ = Z_AV + A_WIDTH
Z_GM = Z_AZ + A_WIDTH
Z_GA = Z_GM + D_MODEL
Z_WIDTH = Z_GA + D_MODEL
QK_WIDTH = 2 * M_WIDTH
W_A_WIDTH = QK_WIDTH + Z_AQ
PROJ_BLOCK = 512
CONV_GROUPS = 4

_NT = (((1,), (1,)), ((), ()))
NEG_LOG2_E = -1.4426950408889634


def _sigmoid(x):
    return 1.0 / (1.0 + jnp.exp2(x * NEG_LOG2_E))


def _silu(x):
    return x * _sigmoid(x)


def _log_sigmoid(x):
    return jnp.minimum(x, 0.0) - jnp.log1p(jnp.exp(-jnp.abs(x)))


def _bf16(x):
    return x.astype(jnp.bfloat16)


def _dot(a, b):
    return jnp.dot(a, b, preferred_element_type=jnp.float32)


def _split3(x):
    hi = _bf16(x)
    r = x - hi.astype(jnp.float32)
    mid = _bf16(r)
    lo = _bf16(r - mid.astype(jnp.float32))
    return hi, mid, lo


def _stage1_pieces(rows, chunk, hb_ref, ng_ref, wa_ref, wb_ref, b_ref, wgt_ref, bgt_ref):
    def norm(x):
        inv = lax.rsqrt(jnp.mean(x * x, axis=-1, keepdims=True) + EPS)
        hb_ref[...] = _bf16(x * inv * ng_ref[...])

    def proj_u(j, u_dst):
        u_dst[:, j:j + PROJ_BLOCK] = _dot(hb_ref[...], wa_ref[:, j:j + PROJ_BLOCK]) + b_ref[:, j:j + PROJ_BLOCK]

    def proj_z(j, z_dst):
        if j < Z_AQ:
            w = wa_ref[:, QK_WIDTH + j:QK_WIDTH + j + PROJ_BLOCK]
        else:
            w = wb_ref[:, j - Z_AQ:j - Z_AQ + PROJ_BLOCK]
        z_dst[:, j:j + PROJ_BLOCK] = _dot(hb_ref[...], w) + b_ref[:, QK_WIDTH + j:QK_WIDTH + j + PROJ_BLOCK]

    def gate_pre():
        g_rows = lax.dot_general(wgt_ref[...], hb_ref[...], _NT,
                                 preferred_element_type=jnp.float32) + bgt_ref[...]
        return g_rows, _log_sigmoid(g_rows)

    def gate_rows(gates_dst, g_rows, lf_all):
        tr = lax.broadcasted_iota(jnp.int32, (rows, rows), 0)
        tc = lax.broadcasted_iota(jnp.int32, (rows, rows), 1)
        same_chunk = jnp.bitwise_xor(tr, tc) < chunk
        cum_mat = jnp.where((tr <= tc) & same_chunk, 1.0, 0.0).astype(jnp.bfloat16)
        hi, mid, lo3 = _split3(lf_all)
        b_all = _dot(hi, cum_mat) + _dot(mid, cum_mat) + _dot(lo3, cum_mat)
        gates_dst[...] = jnp.concatenate(
            [g_rows[0:M_HEADS, :] - b_all[M_HEADS:2 * M_HEADS, :], lf_all[M_HEADS:2 * M_HEADS, :]], axis=0)

    return norm, proj_u, proj_z, gate_pre, gate_rows


def _merge_block(j, z, gmix, gatt, mix_ref, wbm_ref, wba_ref):
    cs = slice(j, j + PROJ_BLOCK)
    u_m = _dot(gmix[...], wbm_ref[:, cs])
    u_a = _dot(gatt[...], wba_ref[:, cs])
    mix_ref[:, cs] = _bf16(_sigmoid(z[:, Z_GM + j:Z_GM + j + PROJ_BLOCK]) * u_m
                           + _sigmoid(z[:, Z_GA + j:Z_GA + j + PROJ_BLOCK]) * u_a)


def _out_block(j, xres, mix_ref, wout_ref, y_dst):
    cs = slice(j, j + PROJ_BLOCK)
    y_dst[:, cs] = xres[:, cs] + _dot(mix_ref[...], wout_ref[:, cs])


def _layer_kernel(*refs, tile, chunk, n_tiles, n_steps, pipelined):
    has_state = not pipelined
    n_chunks = tile // chunk
    band = WINDOW + chunk
    it = iter(refs)
    if pipelined:
        x_ref, x0_ref = next(it), next(it)
        ng_ref, wa_ref, wb_ref, b_ref, wgt_ref, bgt_ref = (next(it) for _ in range(6))
    else:
        u_in_ref, z_in_ref, gates_in_ref = (next(it) for _ in range(3))
        c0_ref, n0_ref, m0_ref, conv0_ref, k0_ref, v0_ref = (next(it) for _ in range(6))
    cw_ref, cb_ref, mg_ref, qg_ref, kg_ref, bias_ref = (next(it) for _ in range(6))
    if pipelined:
        wbm_ref, wba_ref, wout_ref = (next(it) for _ in range(3))
        y_ref = next(it)
    else:
        gmix_out_ref, gatt_out_ref = next(it), next(it)
    c_ref, n_out_ref, m_out_ref, conv_out_ref, k_out_ref, v_out_ref = (next(it) for _ in range(6))
    if pipelined:
        hb_ref, u_ref, z2_ref, qs2_ref, ks2_ref, gates2_ref, xres2_ref = (next(it) for _ in range(7))
        kh_ref, vh_ref, gmix_ref, gatt_ref, mix_ref, nd_ref, m_ref = (next(it) for _ in range(7))
    else:
        u_ref, qs_ref, ks_ref, kh_ref, vh_ref, nd_ref, m_ref = (next(it) for _ in range(7))

    i = pl.program_id(0)
    t2 = i % n_tiles
    if pipelined:
        t1 = jnp.minimum(i + 1, n_steps - 1) % n_tiles
        slot1, slot2 = (i + 1) % 2, i % 2
        z, qs, ks = z2_ref.at[slot2], qs2_ref.at[slot2], ks2_ref.at[slot2]
        gates, xres = gates2_ref.at[slot2], xres2_ref.at[slot2]
        gmix, gatt = gmix_ref, gatt_ref
    else:
        z, qs, ks, gates = z_in_ref.at[0], qs_ref, ks_ref, gates_in_ref.at[0]
        gmix, gatt = gmix_out_ref.at[0], gatt_out_ref.at[0]

    neg_inf = jnp.float32(-jnp.inf)
    lo = lax.broadcasted_iota(jnp.int32, (1, LANES), 1) < A_HEAD_DIM

    def chunk_rows(r):
        return jnp.stack([r[:, c * chunk:(c + 1) * chunk] for c in range(n_chunks)], axis=0)

    def chunked(a):
        return a.reshape(n_chunks, chunk, a.shape[-1])

    def wide(a):
        return jnp.concatenate([a, a], axis=-1)

    def conv(group, qs_dst, ks_dst):
        per = QK_WIDTH // LANES // CONV_GROUPS
        for j in range(group * per * LANES, (group + 1) * per * LANES, LANES):
            cs = slice(j, j + LANES)
            acc = cb_ref[:, cs] + cw_ref[CONV_WIDTH - 1:CONV_WIDTH, cs] * u_ref[SUBLANES:SUBLANES + tile, cs]
            for d in range(1, CONV_WIDTH):
                acc = acc + (cw_ref[CONV_WIDTH - 1 - d:CONV_WIDTH - d, cs]
                             * u_ref[SUBLANES - d:SUBLANES - d + tile, cs])
            act = _silu(acc)
            if j < M_WIDTH:
                qs_dst[:, cs] = act
            else:
                ks_dst[:, j - M_WIDTH:j - M_WIDTH + LANES] = act * (M_HEAD_DIM ** -0.5)

    def conv_tail(t_tile, live):
        @pl.when((t_tile == n_tiles - 1) & live)
        def _conv_out():
            conv_out_ref[0] = u_ref[tile:tile + SUBLANES, :]

        if n_tiles > 1:
            u_ref[0:SUBLANES, :] = u_ref[tile:tile + SUBLANES, :]

    row = lax.broadcasted_iota(jnp.int32, (1, chunk, chunk), 1)
    col = lax.broadcasted_iota(jnp.int32, (1, chunk, chunk), 2)
    lower = row >= col
    ones_v = jnp.ones((n_chunks, chunk, M_HEAD_DIM), jnp.bfloat16)

    def mlstm_intra(h):
        cs = slice(h * M_HEAD_DIM, (h + 1) * M_HEAD_DIM)
        q = chunked(qs[:, cs])
        k = chunked(ks[:, cs])
        qb, kb = _bf16(q), _bf16(k)
        v_aug = jnp.concatenate(
            [_bf16(chunked(z[:, Z_MV + h * M_HEAD_DIM:Z_MV + (h + 1) * M_HEAD_DIM])), ones_v], axis=2)
        a_row = chunk_rows(gates[h:h + 1, :])
        lf_row = chunk_rows(gates[M_HEADS + h:M_HEADS + h + 1, :])
        b_col = jnp.sum(jnp.where(lower, lf_row, 0.0), axis=2, keepdims=True)
        m_loc = jnp.max(jnp.where(lower, a_row, neg_inf), axis=2, keepdims=True)
        decay = jnp.where(lower, jnp.exp(a_row - m_loc), 0.0)
        s = jnp.einsum("ctd,csd->cts", qb, kb, preferred_element_type=jnp.float32) * decay
        kw_t = _bf16(jnp.swapaxes(k, 1, 2) * decay[:, chunk - 1:chunk, :])
        return dict(h=h, qb=qb, v_aug=v_aug, b_col=b_col, m_loc=m_loc, s=_bf16(s), kw_t=kw_t)

    def mlstm_products(st):
        st["pv"] = jnp.einsum("cts,csd->ctd", st.pop("s"), st["v_aug"], preferred_element_type=jnp.float32)
        st["kv"] = jnp.einsum("cks,csd->ckd", st.pop("kw_t"), st.pop("v_aug"), preferred_element_type=jnp.float32)

    def mlstm_scan(st):
        h, m_loc, b_col, kv = st["h"], st["m_loc"], st["b_col"], st.pop("kv")
        m_loc_end = m_loc[:, chunk - 1:chunk, :]
        b_end = b_col[:, chunk - 1:chunk, :]
        cn_run = jnp.concatenate([c_ref[0, h], nd_ref[h]], axis=1)
        m_run = m_ref[h:h + 1, :]
        cn_start, m_start = [], []
        for c in range(n_chunks):
            cn_start.append(_bf16(cn_run))
            m_start.append(m_run)
            m_end = jnp.maximum(m_loc_end[c], m_run)
            cn_run = wide(jnp.exp(m_run - m_end)) * cn_run + wide(jnp.exp(m_loc_end[c] - m_end)) * kv[c]
            m_run = b_end[c] + m_end
        c_ref[0, h] = cn_run[:, 0:M_HEAD_DIM]
        nd_ref[h] = cn_run[:, M_HEAD_DIM:2 * M_HEAD_DIM]
        m_ref[h:h + 1, :] = m_run
        st["m0"] = jnp.stack(m_start, axis=0)
        st["qc"] = jnp.einsum("ctk,ckv->ctv", st.pop("qb"), jnp.stack(cn_start, axis=0),
                              preferred_element_type=jnp.float32)

    def mlstm_out(st):
        h, m_loc, b_col = st["h"], st["m_loc"], st["b_col"]
        cs = slice(h * M_HEAD_DIM, (h + 1) * M_HEAD_DIM)
        m_max = jnp.maximum(m_loc, st["m0"])
        w_intra = jnp.exp(m_loc - m_max)
        w_state = jnp.exp(st["m0"] - m_max)
        both = wide(w_intra) * st["pv"] + wide(w_state) * st["qc"]
        num, den = both[:, :, 0:M_HEAD_DIM], both[:, :, M_HEAD_DIM:2 * M_HEAD_DIM]
        hh = num / jnp.maximum(jnp.abs(den), jnp.exp(-(b_col + m_max)))
        hn = hh * lax.rsqrt(jnp.mean(hh * hh, axis=2, keepdims=True) + EPS) * mg_ref[h:h + 1, :]
        hn = hn.reshape(tile, M_HEAD_DIM)
        hm = _sigmoid(z[:, Z_MO + h * M_HEAD_DIM:Z_MO + (h + 1) * M_HEAD_DIM]) * hn
        gmix[:, cs] = _bf16(hm * _silu(z[:, Z_MZ + h * M_HEAD_DIM:Z_MZ + (h + 1) * M_HEAD_DIM]))

    def head_norm(a, g):
        sq = a * a
        s_lo = jnp.sum(jnp.where(lo, sq, 0.0), axis=1, keepdims=True)
        s_hi = jnp.sum(jnp.where(lo, 0.0, sq), axis=1, keepdims=True)
        ms = jnp.where(lo, s_lo, s_hi) * (1.0 / A_HEAD_DIM)
        return a * lax.rsqrt(ms + EPS) * g

    def band_write(p):
        cs = slice(p * LANES, (p + 1) * LANES)
        kn = head_norm(z[:, Z_AK + p * LANES:Z_AK + (p + 1) * LANES], kg_ref[...])
        vv = z[:, Z_AV + p * LANES:Z_AV + (p + 1) * LANES]
        kh_ref[WINDOW:WINDOW + tile, cs] = _bf16(kn)
        vh_ref[WINDOW:WINDOW + tile, cs] = _bf16(vv)
        if n_tiles > 1:
            ring = pl.multiple_of((t2 * tile) % WINDOW, tile)
            k_out_ref[0, pl.ds(ring, tile), cs] = kn
            v_out_ref[0, pl.ds(ring, tile), cs] = vv
        else:
            k_out_ref[0, :, cs] = kn
            v_out_ref[0, :, cs] = vv

    bands = [slice(c * chunk, c * chunk + band) for c in range(n_chunks)]
    ones_band = jnp.ones((band, LANES), jnp.bfloat16)

    def attention_scores(p):
        cs = slice(p * LANES, (p + 1) * LANES)
        qn = head_norm(z[:, Z_AQ + p * LANES:Z_AQ + (p + 1) * LANES], qg_ref[...])
        qn = chunked(qn * (A_HEAD_DIM ** -0.5))
        q2 = _bf16(jnp.concatenate([jnp.where(lo, qn, 0.0), jnp.where(lo, 0.0, qn)], axis=1))
        s = jnp.stack([lax.dot_general(q2[c], kh_ref[bands[c], cs], _NT, preferred_element_type=jnp.float32)
                       for c in range(n_chunks)], axis=0)
        s = jnp.concatenate([s[:, :, 0:BIAS_FAR], s[:, :, BIAS_FAR:band] + bias_ref[p]], axis=2)
        if not has_state:
            kcol = lax.broadcasted_iota(jnp.int32, (n_chunks, 1, band), 2)
            first_valid = WINDOW - t2 * tile - chunk * lax.broadcasted_iota(jnp.int32, (n_chunks, 1, band), 0)
            s = jnp.where(kcol >= first_valid, s, neg_inf)
        return p, _bf16(jnp.exp(s - jnp.max(s, axis=2, keepdims=True)))

    def attention_values(p, eb):
        cs = slice(p * LANES, (p + 1) * LANES)
        o2 = jnp.stack([_dot(eb[c], jnp.concatenate([vh_ref[bands[c], cs], ones_band], axis=1))
                        for c in range(n_chunks)], axis=0)
        o2 = o2[:, :, 0:LANES] / o2[:, :, LANES:2 * LANES]
        o = jnp.where(lo, o2[:, 0:chunk], o2[:, chunk:2 * chunk]).reshape(tile, LANES)
        gatt[:, cs] = _bf16(o * _silu(z[:, Z_AZ + p * LANES:Z_AZ + (p + 1) * LANES]))

    def attention_all(fillers):
        fillers = list(fillers)
        pending = attention_scores(0)
        for p in range(1, A_PAIRS):
            nxt = attention_scores(p)
            if fillers:
                fillers.pop(0)()
            attention_values(*pending)
            pending = nxt
        if fillers:
            fillers.pop(0)()
        attention_values(*pending)
        for f in fillers:
            f()

    def band_slide():
        for r in range(0, WINDOW, LANES):
            kh_ref[r:r + LANES, :] = kh_ref[tile + r:tile + r + LANES, :]
            vh_ref[r:r + LANES, :] = vh_ref[tile + r:tile + r + LANES, :]

    if pipelined:
        norm, proj_u, proj_z, gate_pre, gate_rows = _stage1_pieces(
            tile, chunk, hb_ref, ng_ref, wa_ref, wb_ref, b_ref, wgt_ref, bgt_ref)
        u_new = u_ref.at[SUBLANES:SUBLANES + tile]

        @pl.when(i == 0)
        def _prologue():
            u_ref[0:SUBLANES, :] = jnp.zeros((SUBLANES, QK_WIDTH), jnp.float32)
            x0 = x0_ref[0]
            xres2_ref[0] = x0
            norm(x0)
            for j in range(0, QK_WIDTH, PROJ_BLOCK):
                proj_u(j, u_new)
            for j in range(0, Z_WIDTH, PROJ_BLOCK):
                proj_z(j, z2_ref.at[0])
            gate_rows(gates2_ref.at[0], *gate_pre())
            for g in range(CONV_GROUPS):
                conv(g, qs2_ref.at[0], ks2_ref.at[0])
            conv_tail(0, True)

    @pl.when(t2 == 0)
    def _init_stream():
        if has_state:
            c_ref[0] = c0_ref[0]
            for h in range(M_HEADS):
                nd_ref[h] = jnp.broadcast_to(n0_ref[0, h:h + 1, :], (M_HEAD_DIM, M_HEAD_DIM)).T
            m_ref[0:M_HEADS, :] = m0_ref[0]
            kh_ref[0:WINDOW, :] = k0_ref[0]
            vh_ref[0:WINDOW, :] = v0_ref[0]
        else:
            c_ref[...] = jnp.zeros_like(c_ref)
            nd_ref[...] = jnp.zeros_like(nd_ref)
            m_ref[...] = jnp.zeros_like(m_ref)
            kh_ref[0:WINDOW, :] = jnp.zeros((WINDOW, A_WIDTH), jnp.bfloat16)
            vh_ref[0:WINDOW, :] = jnp.zeros((WINDOW, A_WIDTH), jnp.bfloat16)

    if pipelined:
        @pl.when(t1 == 0)
        def _init_conv_tail():
            u_ref[0:SUBLANES, :] = jnp.zeros((SUBLANES, QK_WIDTH), jnp.float32)

        z1, qs1, ks1 = z2_ref.at[slot1], qs2_ref.at[slot1], ks2_ref.at[slot1]
        blocks = iter(range(0, Z_WIDTH, PROJ_BLOCK))
        x1 = x_ref[0]
        xres2_ref[slot1] = x1
        norm(x1)
        group_a = [mlstm_intra(h) for h in range(0, M_HEADS // 2)]
        proj_u(0, u_new)
        proj_u(PROJ_BLOCK, u_new)
        gate_pre_acts = gate_pre()
        for st in group_a:
            mlstm_products(st)
        proj_z(next(blocks), z1)
        group_b = [mlstm_intra(h) for h in range(M_HEADS // 2, M_HEADS)]
        for st in group_a:
            mlstm_scan(st)
        proj_z(next(blocks), z1)
        for st in group_b:
            mlstm_products(st)
        proj_z(next(blocks), z1)
        for st in group_b:
            mlstm_scan(st)
        gate_rows(gates2_ref.at[slot1], *gate_pre_acts)
        for p in range(A_PAIRS):
            band_write(p)
        for st in group_a + group_b:
            mlstm_out(st)
            proj_z(next(blocks), z1)
        attention_all([functools.partial(proj_z, j, z1) for j in blocks])
        band_slide()
        for g, j in enumerate(range(0, D_MODEL, PROJ_BLOCK)):
            _merge_block(j, z, gmix, gatt, mix_ref, wbm_ref, wba_ref)
            conv(g, qs1, ks1)
        for g, j in enumerate(range(0, D_MODEL, PROJ_BLOCK)):
            _out_block(j, xres, mix_ref, wout_ref, y_ref.at[0])
            conv(D_MODEL // PROJ_BLOCK + g, qs1, ks1)
        conv_tail(t1, i + 1 < n_steps)
    else:
        u_ref[0:SUBLANES, :] = conv0_ref[0]
        u_ref[SUBLANES:SUBLANES + tile, :] = u_in_ref[0]
        for g in range(CONV_GROUPS):
            conv(g, qs, ks)
        conv_tail(t2, True)
        heads = [mlstm_intra(h) for h in range(M_HEADS)]
        for st in heads:
            mlstm_products(st)
        for st in heads:
            mlstm_scan(st)
        for p in range(A_PAIRS):
            band_write(p)
        for st in heads:
            mlstm_out(st)
        attention_all([])

    @pl.when(t2 == n_tiles - 1)
    def _state_out():
        for h in range(M_HEADS):
            n_out_ref[0, h:h + 1, :] = nd_ref[h].T[0:1, :]
        m_out_ref[0] = m_ref[0:M_HEADS, :]


def _project_kernel(x_ref, ng_ref, wa_ref, wb_ref, b_ref, wgt_ref, bgt_ref, u_ref, z_ref, gates_ref, hb_ref,
                    *, rows, chunk):
    norm, proj_u, proj_z, gate_pre, gate_rows = _stage1_pieces(
        rows, chunk, hb_ref, ng_ref, wa_ref, wb_ref, b_ref, wgt_ref, bgt_ref)
    norm(x_ref[...])
    for j in range(0, QK_WIDTH, PROJ_BLOCK):
        proj_u(j, u_ref)
    for j in range(0, Z_WIDTH, PROJ_BLOCK):
        proj_z(j, z_ref)
    gate_rows(gates_ref, *gate_pre())


def _merge_kernel(gmix_ref, gatt_ref, z_ref, x_ref, wbm_ref, wba_ref, wout_ref, y_ref, mix_ref):
    for j in range(0, D_MODEL, PROJ_BLOCK):
        _merge_block(j, z_ref, gmix_ref, gatt_ref, mix_ref, wbm_ref, wba_ref)
    for j in range(0, D_MODEL, PROJ_BLOCK):
        _out_block(j, x_ref, mix_ref, wout_ref, y_ref)


def _const_spec(shape):
    zeros = (0,) * len(shape)
    return pl.BlockSpec(shape, lambda i: zeros, pipeline_mode=pl.Buffered(1))


def _full_spec(shape):
    zeros = (0,) * len(shape)
    return pl.BlockSpec(shape, lambda i: zeros)


def _per_stream_spec(shape, n_tiles=1):
    nd = len(shape)
    return pl.BlockSpec((1,) + tuple(shape[1:]), lambda i: (i // n_tiles,) + (0,) * (nd - 1))


def _state_out_shapes(n_streams, keep):
    f32 = jnp.float32
    return (
        jax.ShapeDtypeStruct((n_streams, M_HEADS, M_HEAD_DIM, M_HEAD_DIM), f32),
        jax.ShapeDtypeStruct((n_streams, M_HEADS, LANES), f32),
        jax.ShapeDtypeStruct((n_streams, M_HEADS, LANES), f32),
        jax.ShapeDtypeStruct((n_streams, SUBLANES, QK_WIDTH), f32),
        jax.ShapeDtypeStruct((n_streams, keep, A_WIDTH), f32),
        jax.ShapeDtypeStruct((n_streams, keep, A_WIDTH), f32),
    )


def _compiler_params():
    return pltpu.CompilerParams(dimension_semantics=("arbitrary",), vmem_limit_bytes=VMEM_LIMIT_BYTES)


def _fresh_layer_call(x, stage1_params, mixer_params, merge_params, *, tile, chunk):
    n_streams, seq, _ = x.shape
    n_tiles = seq // tile
    n_steps = n_streams * n_tiles
    assert n_tiles > 1 and WINDOW % tile == 0 and seq % WINDOW == 0

    def tile_spec(step_of):
        return pl.BlockSpec((1, tile, D_MODEL), lambda i: (step_of(i) // n_tiles, step_of(i) % n_tiles, 0))

    params = tuple(stage1_params) + tuple(mixer_params) + tuple(merge_params)
    in_specs = [tile_spec(lambda i: jnp.minimum(i + 1, n_steps - 1)),
                pl.BlockSpec((1, tile, D_MODEL), lambda i: (0, 0, 0), pipeline_mode=pl.Buffered(1))]
    in_specs += [_const_spec(a.shape) for a in params]
    out_shape = (jax.ShapeDtypeStruct(x.shape, jnp.float32),) + _state_out_shapes(n_streams, WINDOW)
    out_specs = (tile_spec(lambda i: i),) + tuple(_per_stream_spec(s.shape, n_tiles) for s in out_shape[1:])
    f32, bf16 = jnp.float32, jnp.bfloat16
    scratch = [
        pltpu.VMEM((tile, D_MODEL), bf16),
        pltpu.VMEM((tile + SUBLANES, QK_WIDTH), f32),
        pltpu.VMEM((2, tile, Z_WIDTH), f32),
        pltpu.VMEM((2, tile, M_WIDTH), f32),
        pltpu.VMEM((2, tile, M_WIDTH), f32),
        pltpu.VMEM((2, 2 * M_HEADS, tile), f32),
        pltpu.VMEM((2, tile, D_MODEL), f32),
        pltpu.VMEM((WINDOW + tile, A_WIDTH), bf16),
        pltpu.VMEM((WINDOW + tile, A_WIDTH), bf16),
        pltpu.VMEM((tile, M_WIDTH), bf16),
        pltpu.VMEM((tile, A_WIDTH), bf16),
        pltpu.VMEM((tile, D_MODEL), bf16),
        pltpu.VMEM((M_HEADS, M_HEAD_DIM, M_HEAD_DIM), f32),
        pltpu.VMEM((SUBLANES, LANES), f32),
    ]
    kern = functools.partial(_layer_kernel, tile=tile, chunk=chunk, n_tiles=n_tiles, n_steps=n_steps,
                             pipelined=True)
    return pl.pallas_call(
        kern, grid=(n_steps,), in_specs=in_specs, out_specs=out_specs, out_shape=out_shape,
        scratch_shapes=scratch, compiler_params=_compiler_params(), name="layer_fresh",
    )(x, x, *params)


def _running_layer_call(x, state, stage1_params, mixer_params, merge_params):
    n_streams, seq, _ = x.shape
    rows = n_streams * seq
    f32, bf16 = jnp.float32, jnp.bfloat16
    x_rows = x.reshape(rows, D_MODEL)

    u, z, gate_rows = pl.pallas_call(
        functools.partial(_project_kernel, rows=rows, chunk=seq),
        grid=(1,),
        in_specs=[_const_spec(x_rows.shape)] + [_const_spec(a.shape) for a in stage1_params],
        out_specs=(_full_spec((rows, QK_WIDTH)), _full_spec((rows, Z_WIDTH)), _full_spec((2 * M_HEADS, rows))),
        out_shape=(jax.ShapeDtypeStruct((rows, QK_WIDTH), f32), jax.ShapeDtypeStruct((rows, Z_WIDTH), f32),
                   jax.ShapeDtypeStruct((2 * M_HEADS, rows), f32)),
        scratch_shapes=[pltpu.VMEM((rows, D_MODEL), bf16)],
        compiler_params=_compiler_params(), name="running_project",
    )(x_rows, *stage1_params)

    per_stream_in = (u.reshape(n_streams, seq, QK_WIDTH), z.reshape(n_streams, seq, Z_WIDTH),
                     gate_rows.reshape(2 * M_HEADS, n_streams, seq).transpose(1, 0, 2)) + tuple(state)
    out_shape = (jax.ShapeDtypeStruct((n_streams, seq, M_WIDTH), bf16),
                 jax.ShapeDtypeStruct((n_streams, seq, A_WIDTH), bf16)) + _state_out_shapes(n_streams, seq)
    scratch = [
        pltpu.VMEM((seq + SUBLANES, QK_WIDTH), f32),
        pltpu.VMEM((seq, M_WIDTH), f32),
        pltpu.VMEM((seq, M_WIDTH), f32),
        pltpu.VMEM((WINDOW + seq, A_WIDTH), bf16),
        pltpu.VMEM((WINDOW + seq, A_WIDTH), bf16),
        pltpu.VMEM((M_HEADS, M_HEAD_DIM, M_HEAD_DIM), f32),
        pltpu.VMEM((SUBLANES, LANES), f32),
    ]
    outs = pl.pallas_call(
        functools.partial(_layer_kernel, tile=seq, chunk=seq, n_tiles=1, n_steps=n_streams, pipelined=False),
        grid=(n_streams,),
        in_specs=[_per_stream_spec(a.shape) for a in per_stream_in] + [_const_spec(a.shape) for a in mixer_params],
        out_specs=tuple(_per_stream_spec(s.shape) for s in out_shape),
        out_shape=out_shape, scratch_shapes=scratch,
        compiler_params=_compiler_params(), name="running_mixers",
    )(*per_stream_in, *mixer_params)
    gmix, gatt = outs[0].reshape(rows, M_WIDTH), outs[1].reshape(rows, A_WIDTH)

    merge_in = (gmix, gatt, z, x_rows) + tuple(merge_params)
    y = pl.pallas_call(
        _merge_kernel, grid=(1,),
        in_specs=[_const_spec(a.shape) for a in merge_in],
        out_specs=_full_spec((rows, D_MODEL)),
        out_shape=jax.ShapeDtypeStruct((rows, D_MODEL), f32),
        scratch_shapes=[pltpu.VMEM((rows, D_MODEL), bf16)],
        compiler_params=_compiler_params(), name="running_merge",
    )(*merge_in)
    return (y.reshape(x.shape),) + tuple(outs[2:])


def _pair_bias(rel_bias, chunk):
    assert chunk - 1 <= REL_CLIP <= WINDOW and chunk + REL_CLIP <= 2 * REL_CLIP + 1
    width = WINDOW + chunk - BIAS_FAR
    n_ext = width + chunk - 1
    rev = rel_bias[:, ::-1].astype(jnp.float32)
    ext = jnp.concatenate([jnp.broadcast_to(rev[:, :1], (A_HEADS, chunk - 1)), rev[:, :width]], axis=1)
    ext = ext - rev[:, :1]
    flat = jnp.tile(ext, (1, chunk))[:, chunk - 1:chunk - 1 + chunk * (n_ext - 1)]
    bias = flat.reshape(A_HEADS, chunk, n_ext - 1)[:, :, :width]
    return bias.reshape(A_PAIRS, 2 * chunk, width)


def _layer_params(norm_g, w_in, b_in, conv_w, conv_b, m_head_g, q_norm_g, k_norm_g, w_bm, w_ba, w_out):
    gate_lo = W_A_WIDTH
    gate_hi = gate_lo + 2 * M_HEADS
    w_a = w_in[:, :gate_lo].astype(jnp.bfloat16)
    w_b = w_in[:, gate_hi:].astype(jnp.bfloat16)
    b_main = jnp.concatenate([b_in[:gate_lo], b_in[gate_hi:]])[None, :]
    w_gate_t = w_in[:, gate_lo:gate_hi].T.astype(jnp.bfloat16)
    stage1 = (norm_g[None, :], w_a, w_b, b_main, w_gate_t, b_in[gate_lo:gate_hi, None])
    mixers = (conv_w, conv_b[None, :], m_head_g, jnp.tile(q_norm_g, 2)[None, :], jnp.tile(k_norm_g, 2)[None, :])
    merge = (w_bm.astype(jnp.bfloat16), w_ba.astype(jnp.bfloat16), w_out.astype(jnp.bfloat16))
    return stage1, mixers, merge


PROMPT_TILE = 256


def kernel(x_prompt, x_sample, state_mlstm_C, state_mlstm_n, state_mlstm_m, state_mlstm_conv,
           cache_attn_k, cache_attn_v, norm_g, w_in, b_in, conv_w, conv_b, m_head_g,
           q_norm_g, k_norm_g, rel_bias, w_bm, w_ba, w_out):
    assert tuple(int(w) for w in IN_WIDTHS[:5]) == (M_WIDTH,) * 5 and sum(IN_WIDTHS) == w_in.shape[-1]
    depth = w_in.shape[0]
    xp, xs = x_prompt, x_sample
    n_p = xp.shape[0]
    n_s, t_s, _ = xs.shape

    def unpack(c, n, m, conv, k, v):
        n_streams = c.shape[0]
        return (c, n, m[:, :, 0], conv[:, SUBLANES - (CONV_WIDTH - 1):],
                k.reshape(n_streams, -1, A_HEADS, A_HEAD_DIM), v.reshape(n_streams, -1, A_HEADS, A_HEAD_DIM))

    outs_p, outs_s = [], []
    for l in range(depth):
        stage1, mixers, merge = _layer_params(norm_g[l], w_in[l], b_in[l], conv_w[l], conv_b[l], m_head_g[l],
                                              q_norm_g[l], k_norm_g[l], w_bm[l], w_ba[l], w_out[l])
        xp, *state_p = _fresh_layer_call(xp, stage1, mixers + (_pair_bias(rel_bias[l], CHUNK),), merge,
                                         tile=PROMPT_TILE, chunk=CHUNK)
        outs_p.append(unpack(*state_p))
        state = (state_mlstm_C[l], state_mlstm_n[l],
                 jnp.broadcast_to(state_mlstm_m[l][:, :, None], (n_s, M_HEADS, LANES)),
                 jnp.pad(state_mlstm_conv[l], ((0, 0), (SUBLANES - (CONV_WIDTH - 1), 0), (0, 0))),
                 cache_attn_k[l].reshape(n_s, -1, A_WIDTH).astype(jnp.bfloat16),
                 cache_attn_v[l].reshape(n_s, -1, A_WIDTH).astype(jnp.bfloat16))
        xs, *state_s = _running_layer_call(xs, state, stage1, mixers + (_pair_bias(rel_bias[l], t_s),), merge)
        outs_s.append(unpack(*state_s))
    stack = lambda outs, i: jnp.stack([o[i] for o in outs])
    return (xp, xs) + tuple(stack(outs_p, i) for i in range(6)) + tuple(stack(outs_s, i) for i in range(6))
```

```python
import functools

import jax
import jax.numpy as jnp
import numpy as np
from jax import lax
from jax.experimental import pallas as pl
from jax.experimental.pallas import tpu as pltpu

D_MODEL = 1024
CHUNK = 64
M_HEADS = 4
M_HEAD_DIM = 128
M_WIDTH = M_HEADS * M_HEAD_DIM
CONV_WIDTH = 4
A_HEADS = 8
A_HEAD_DIM = 64
A_WIDTH = A_HEADS * A_HEAD_DIM
A_PAIRS = A_HEADS // 2
WINDOW = 8 * CHUNK
REL_CLIP = 128
BIAS_FAR = WINDOW - REL_CLIP
EPS = 1e-6
IN_WIDTHS = (M_WIDTH, M_WIDTH, M_WIDTH, M_WIDTH, M_WIDTH, M_HEADS, M_HEADS,
             A_WIDTH, A_WIDTH, A_WIDTH, A_WIDTH, D_MODEL, D_MODEL)

LANES = 128
SUBLANES = 8
VMEM_LIMIT_BYTES = 60 * 1024 * 1024

Z_MV = 0
Z_MO = Z_MV + M_WIDTH
Z_MZ = Z_MO + M_WIDTH
Z_AQ = Z_MZ + M_WIDTH
Z_AK = Z_AQ + A_WIDTH
Z_AV = Z_AK + A_WIDTH
Z_AZ = Z_AV + A_WIDTH
Z_GM = Z_AZ + A_WIDTH
Z_GA = Z_GM + D_MODEL
Z_WIDTH = Z_GA + D_MODEL
QK_WIDTH = 2 * M_WIDTH
W_A_WIDTH = QK_WIDTH + Z_AQ
PROJ_BLOCK = 512
CONV_GROUPS = 4

_NT = (((1,), (1,)), ((), ()))


def _sigmoid(x):
    return 1.0 / (1.0 + jnp.exp(-x))


def _silu(x):
    return x * _sigmoid(x)


def _log_sigmoid(x):
    return jnp.minimum(x, 0.0) - jnp.log1p(jnp.exp(-jnp.abs(x)))


def _bf16(x):
    return x.astype(jnp.bfloat16)


def _dot(a, b):
    return jnp.dot(a, b, preferred_element_type=jnp.float32)


def _split3(x):
    hi = _bf16(x)
    r = x - hi.astype(jnp.float32)
    mid = _bf16(r)
    lo = _bf16(r - mid.astype(jnp.float32))
    return hi, mid, lo


def _stage1_pieces(rows, chunk, hb_ref, ng_ref, wa_ref, wb_ref, b_ref, wgt_ref, bgt_ref):
    def norm(x):
        inv = lax.rsqrt(jnp.mean(x * x, axis=-1, keepdims=True) + EPS)
        hb_ref[...] = _bf16(x * inv * ng_ref[...])

    def proj_u(j, u_dst):
        u_dst[:, j:j + PROJ_BLOCK] = _dot(hb_ref[...], wa_ref[:, j:j + PROJ_BLOCK]) + b_ref[:, j:j + PROJ_BLOCK]

    def proj_z(j, z_dst):
        if j < Z_AQ:
            w = wa_ref[:, QK_WIDTH + j:QK_WIDTH + j + PROJ_BLOCK]
        else:
            w = wb_ref[:, j - Z_AQ:j - Z_AQ + PROJ_BLOCK]
        z_dst[:, j:j + PROJ_BLOCK] = _dot(hb_ref[...], w) + b_ref[:, QK_WIDTH + j:QK_WIDTH + j + PROJ_BLOCK]

    def gate_pre():
        g_rows = lax.dot_general(wgt_ref[...], hb_ref[...], _NT,
                                 preferred_element_type=jnp.float32) + bgt_ref[...]
        return g_rows, _log_sigmoid(g_rows)

    def gate_rows(gates_dst, g_rows, lf_all):
        tr = lax.broadcasted_iota(jnp.int32, (rows, rows), 0)
        tc = lax.broadcasted_iota(jnp.int32, (rows, rows), 1)
        same_chunk = jnp.bitwise_xor(tr, tc) < chunk
        cum_mat = jnp.where((tr <= tc) & same_chunk, 1.0, 0.0).astype(jnp.bfloat16)
        hi, mid, lo3 = _split3(lf_all)
        b_all = _dot(hi, cum_mat) + _dot(mid, cum_mat) + _dot(lo3, cum_mat)
        gates_dst[...] = jnp.concatenate(
            [g_rows[0:M_HEADS, :] - b_all[M_HEADS:2 * M_HEADS, :], lf_all[M_HEADS:2 * M_HEADS, :]], axis=0)

    return norm, proj_u, proj_z, gate_pre, gate_rows


def _merge_block(j, z, gmix, gatt, mix_ref, wbm_ref, wba_ref):
    cs = slice(j, j + PROJ_BLOCK)
    u_m = _dot(gmix[...], wbm_ref[:, cs])
    u_a = _dot(gatt[...], wba_ref[:, cs])
    mix_ref[:, cs] = _bf16(_sigmoid(z[:, Z_GM + j:Z_GM + j + PROJ_BLOCK]) * u_m
                           + _sigmoid(z[:, Z_GA + j:Z_GA + j + PROJ_BLOCK]) * u_a)


def _out_block(j, xres, mix_ref, wout_ref, y_dst):
    cs = slice(j, j + PROJ_BLOCK)
    y_dst[:, cs] = xres[:, cs] + _dot(mix_ref[...], wout_ref[:, cs])


def _layer_kernel(*refs, tile, chunk, n_tiles, n_steps, pipelined):
    has_state = not pipelined
    n_chunks = tile // chunk
    band = WINDOW + chunk
    it = iter(refs)
    if pipelined:
        x_ref, x0_ref = next(it), next(it)
        ng_ref, wa_ref, wb_ref, b_ref, wgt_ref, bgt_ref = (next(it) for _ in range(6))
    else:
        u_in_ref, z_in_ref, gates_in_ref = (next(it) for _ in range(3))
        c0_ref, n0_ref, m0_ref, conv0_ref, k0_ref, v0_ref = (next(it) for _ in range(6))
    cw_ref, cb_ref, mg_ref, qg_ref, kg_ref, bias_ref = (next(it) for _ in range(6))
    if pipelined:
        wbm_ref, wba_ref, wout_ref = (next(it) for _ in range(3))
        y_ref = next(it)
    else:
        gmix_out_ref, gatt_out_ref = next(it), next(it)
    c_ref, n_out_ref, m_out_ref, conv_out_ref, k_out_ref, v_out_ref = (next(it) for _ in range(6))
    if pipelined:
        hb_ref, u_ref, z2_ref, qs2_ref, ks2_ref, gates2_ref, xres2_ref = (next(it) for _ in range(7))
        kh_ref, vh_ref, gmix_ref, gatt_ref, mix_ref, nd_ref, m_ref = (next(it) for _ in range(7))
    else:
        u_ref, qs_ref, ks_ref, kh_ref, vh_ref, nd_ref, m_ref = (next(it) for _ in range(7))

    i = pl.program_id(0)
    t2 = i % n_tiles
    if pipelined:
        t1 = jnp.minimum(i + 1, n_steps - 1) % n_tiles
        z = qs = ks = gates = xres = None
        gmix, gatt = gmix_ref, gatt_ref
    else:
        z, qs, ks, gates = z_in_ref.at[0], qs_ref, ks_ref, gates_in_ref.at[0]
        gmix, gatt = gmix_out_ref.at[0], gatt_out_ref.at[0]

    neg_inf = jnp.float32(-jnp.inf)
    lo = lax.broadcasted_iota(jnp.int32, (1, LANES), 1) < A_HEAD_DIM

    def chunk_rows(r):
        return jnp.stack([r[:, c * chunk:(c + 1) * chunk] for c in range(n_chunks)], axis=0)

    def chunked(a):
        return a.reshape(n_chunks, chunk, a.shape[-1])

    def wide(a):
        return jnp.concatenate([a, a], axis=-1)

    def conv(group, qs_dst, ks_dst):
        per = QK_WIDTH // LANES // CONV_GROUPS
        for j in range(group * per * LANES, (group + 1) * per * LANES, LANES):
            cs = slice(j, j + LANES)
            acc = cb_ref[:, cs] + cw_ref[CONV_WIDTH - 1:CONV_WIDTH, cs] * u_ref[SUBLANES:SUBLANES + tile, cs]
            for d in range(1, CONV_WIDTH):
                acc = acc + (cw_ref[CONV_WIDTH - 1 - d:CONV_WIDTH - d, cs]
                             * u_ref[SUBLANES - d:SUBLANES - d + tile, cs])
            act = _silu(acc)
            if j < M_WIDTH:
                qs_dst[:, cs] = act
            else:
                ks_dst[:, j - M_WIDTH:j - M_WIDTH + LANES] = act * (M_HEAD_DIM ** -0.5)

    def conv_tail(t_tile, live):
        @pl.when((t_tile == n_tiles - 1) & live)
        def _conv_out():
            conv_out_ref[0] = u_ref[tile:tile + SUBLANES, :]

        if n_tiles > 1:
            u_ref[0:SUBLANES, :] = u_ref[tile:tile + SUBLANES, :]

    row = lax.broadcasted_iota(jnp.int32, (1, chunk, chunk), 1)
    col = lax.broadcasted_iota(jnp.int32, (1, chunk, chunk), 2)
    lower = row >= col
    ones_v = jnp.ones((n_chunks, chunk, M_HEAD_DIM), jnp.bfloat16)

    def mlstm_intra(h):
        cs = slice(h * M_HEAD_DIM, (h + 1) * M_HEAD_DIM)
        q = chunked(qs[:, cs])
        k = chunked(ks[:, cs])
        qb, kb = _bf16(q), _bf16(k)
        v_aug = jnp.concatenate(
            [_bf16(chunked(z[:, Z_MV + h * M_HEAD_DIM:Z_MV + (h + 1) * M_HEAD_DIM])), ones_v], axis=2)
        a_row = chunk_rows(gates[h:h + 1, :])
        lf_row = chunk_rows(gates[M_HEADS + h:M_HEADS + h + 1, :])
        b_col = jnp.sum(jnp.where(lower, lf_row, 0.0), axis=2, keepdims=True)
        m_loc = jnp.max(jnp.where(lower, a_row, neg_inf), axis=2, keepdims=True)
        decay = jnp.where(lower, jnp.exp(a_row - m_loc), 0.0)
        s = jnp.einsum("ctd,csd->cts", qb, kb, preferred_element_type=jnp.float32) * decay
        kw_t = _bf16(jnp.swapaxes(k, 1, 2) * decay[:, chunk - 1:chunk, :])
        return dict(h=h, qb=qb, v_aug=v_aug, b_col=b_col, m_loc=m_loc, s=_bf16(s), kw_t=kw_t)

    def mlstm_products(st):
        st["pv"] = jnp.einsum("cts,csd->ctd", st.pop("s"), st["v_aug"], preferred_element_type=jnp.float32)
        st["kv"] = jnp.einsum("cks,csd->ckd", st.pop("kw_t"), st.pop("v_aug"), preferred_element_type=jnp.float32)

    def mlstm_scan(st):
        h, m_loc, b_col, kv = st["h"], st["m_loc"], st["b_col"], st.pop("kv")
        m_loc_end = m_loc[:, chunk - 1:chunk, :]
        b_end = b_col[:, chunk - 1:chunk, :]
        cn_run = jnp.concatenate([c_ref[0, h], nd_ref[h]], axis=1)
        m_run = m_ref[h:h + 1, :]
        cn_start, m_start = [], []
        for c in range(n_chunks):
            cn_start.append(_bf16(cn_run))
            m_start.append(m_run)
            m_end = jnp.maximum(m_loc_end[c], m_run)
            cn_run = wide(jnp.exp(m_run - m_end)) * cn_run + wide(jnp.exp(m_loc_end[c] - m_end)) * kv[c]
            m_run = b_end[c] + m_end
        c_ref[0, h] = cn_run[:, 0:M_HEAD_DIM]
        nd_ref[h] = cn_run[:, M_HEAD_DIM:2 * M_HEAD_DIM]
        m_ref[h:h + 1, :] = m_run
        st["m0"] = jnp.stack(m_start, axis=0)
        st["qc"] = jnp.einsum("ctk,ckv->ctv", st.pop("qb"), jnp.stack(cn_start, axis=0),
                              preferred_element_type=jnp.float32)

    def mlstm_out(st):
        h, m_loc, b_col = st["h"], st["m_loc"], st["b_col"]
        cs = slice(h * M_HEAD_DIM, (h + 1) * M_HEAD_DIM)
        m_max = jnp.maximum(m_loc, st["m0"])
        w_intra = jnp.exp(m_loc - m_max)
        w_state = jnp.exp(st["m0"] - m_max)
        both = wide(w_intra) * st["pv"] + wide(w_state) * st["qc"]
        num, den = both[:, :, 0:M_HEAD_DIM], both[:, :, M_HEAD_DIM:2 * M_HEAD_DIM]
        hh = num / jnp.maximum(jnp.abs(den), jnp.exp(-(b_col + m_max)))
        hn = hh * lax.rsqrt(jnp.mean(hh * hh, axis=2, keepdims=True) + EPS) * mg_ref[h:h + 1, :]
        hn = hn.reshape(tile, M_HEAD_DIM)
        hm = _sigmoid(z[:, Z_MO + h * M_HEAD_DIM:Z_MO + (h + 1) * M_HEAD_DIM]) * hn
        gmix[:, cs] = _bf16(hm * _silu(z[:, Z_MZ + h * M_HEAD_DIM:Z_MZ + (h + 1) * M_HEAD_DIM]))

    def head_norm(a, g):
        sq = a * a
        s_lo = jnp.sum(jnp.where(lo, sq, 0.0), axis=1, keepdims=True)
        s_hi = jnp.sum(jnp.where(lo, 0.0, sq), axis=1, keepdims=True)
        ms = jnp.where(lo, s_lo, s_hi) * (1.0 / A_HEAD_DIM)
        return a * lax.rsqrt(ms + EPS) * g

    def band_write(p):
        cs = slice(p * LANES, (p + 1) * LANES)
        kn = head_norm(z[:, Z_AK + p * LANES:Z_AK + (p + 1) * LANES], kg_ref[...])
        vv = z[:, Z_AV + p * LANES:Z_AV + (p + 1) * LANES]
        kh_ref[WINDOW:WINDOW + tile, cs] = _bf16(kn)
        vh_ref[WINDOW:WINDOW + tile, cs] = _bf16(vv)
        if n_tiles > 1:
            ring = pl.multiple_of((t2 * tile) % WINDOW, tile)
            k_out_ref[0, pl.ds(ring, tile), cs] = kn
            v_out_ref[0, pl.ds(ring, tile), cs] = vv
        else:
            k_out_ref[0, :, cs] = kn
            v_out_ref[0, :, cs] = vv

    bands = [slice(c * chunk, c * chunk + band) for c in range(n_chunks)]
    ones_band = jnp.ones((band, LANES), jnp.bfloat16)

    def attention_scores(p):
        cs = slice(p * LANES, (p + 1) * LANES)
        qn = head_norm(z[:, Z_AQ + p * LANES:Z_AQ + (p + 1) * LANES], qg_ref[...])
        qn = chunked(qn * (A_HEAD_DIM ** -0.5))
        q2 = _bf16(jnp.concatenate([jnp.where(lo, qn, 0.0), jnp.where(lo, 0.0, qn)], axis=1))
        s = jnp.stack([lax.dot_general(q2[c], kh_ref[bands[c], cs], _NT, preferred_element_type=jnp.float32)
                       for c in range(n_chunks)], axis=0)
        s = jnp.concatenate([s[:, :, 0:BIAS_FAR], s[:, :, BIAS_FAR:band] + bias_ref[p]], axis=2)
        if not has_state:
            kcol = lax.broadcasted_iota(jnp.int32, (n_chunks, 1, band), 2)
            first_valid = WINDOW - t2 * tile - chunk * lax.broadcasted_iota(jnp.int32, (n_chunks, 1, band), 0)
            s = jnp.where(kcol >= first_valid, s, neg_inf)
        return p, _bf16(jnp.exp(s - jnp.max(s, axis=2, keepdims=True)))

    def attention_values(p, eb):
        cs = slice(p * LANES, (p + 1) * LANES)
        o2 = jnp.stack([_dot(eb[c], jnp.concatenate([vh_ref[bands[c], cs], ones_band], axis=1))
                        for c in range(n_chunks)], axis=0)
        o2 = o2[:, :, 0:LANES] / o2[:, :, LANES:2 * LANES]
        o = jnp.where(lo, o2[:, 0:chunk], o2[:, chunk:2 * chunk]).reshape(tile, LANES)
        gatt[:, cs] = _bf16(o * _silu(z[:, Z_AZ + p * LANES:Z_AZ + (p + 1) * LANES]))

    def attention_all(fillers):
        fillers = list(fillers)
        pending = None
        for p in range(A_PAIRS):
            nxt = attention_scores(p)
            if pending is not None:
                attention_values(*pending)
            pending = nxt
            if fillers:
                fillers.pop(0)()
        attention_values(*pending)
        for f in fillers:
            f()

    def band_slide():
        for r in range(0, WINDOW, LANES):
            kh_ref[r:r + LANES, :] = kh_ref[tile + r:tile + r + LANES, :]
            vh_ref[r:r + LANES, :] = vh_ref[tile + r:tile + r + LANES, :]

    if pipelined:
        norm, proj_u, proj_z, gate_pre, gate_rows = _stage1_pieces(
            tile, chunk, hb_ref, ng_ref, wa_ref, wb_ref, b_ref, wgt_ref, bgt_ref)
        u_new = u_ref.at[SUBLANES:SUBLANES + tile]

        @pl.when(i == 0)
        def _prologue():
            u_ref[0:SUBLANES, :] = jnp.zeros((SUBLANES, QK_WIDTH), jnp.float32)
            x0 = x0_ref[0]
            xres2_ref[0] = x0
            norm(x0)
            for j in range(0, QK_WIDTH, PROJ_BLOCK):
                proj_u(j, u_new)
            for j in range(0, Z_WIDTH, PROJ_BLOCK):
                proj_z(j, z2_ref.at[0])
            gate_rows(gates2_ref.at[0], *gate_pre())
            for g in range(CONV_GROUPS):
                conv(g, qs2_ref.at[0], ks2_ref.at[0])
            conv_tail(0, True)

    @pl.when(t2 == 0)
    def _init_stream():
        if has_state:
            c_ref[0] = c0_ref[0]
            for h in range(M_HEADS):
                nd_ref[h] = jnp.broadcast_to(n0_ref[0, h:h + 1, :], (M_HEAD_DIM, M_HEAD_DIM)).T
            m_ref[0:M_HEADS, :] = m0_ref[0]
            kh_ref[0:WINDOW, :] = _bf16(k0_ref[0])
            vh_ref[0:WINDOW, :] = _bf16(v0_ref[0])
        else:
            c_ref[...] = jnp.zeros_like(c_ref)
            nd_ref[...] = jnp.zeros_like(nd_ref)
            m_ref[...] = jnp.zeros_like(m_ref)
            kh_ref[0:WINDOW, :] = jnp.zeros((WINDOW, A_WIDTH), jnp.bfloat16)
            vh_ref[0:WINDOW, :] = jnp.zeros((WINDOW, A_WIDTH), jnp.bfloat16)

    if pipelined:
        @pl.when(t1 == 0)
        def _init_conv_tail():
            u_ref[0:SUBLANES, :] = jnp.zeros((SUBLANES, QK_WIDTH), jnp.float32)

        def pipelined_step(slot1, slot2):
            nonlocal z, qs, ks, gates, xres
            z, qs, ks = z2_ref.at[slot2], qs2_ref.at[slot2], ks2_ref.at[slot2]
            gates, xres = gates2_ref.at[slot2], xres2_ref.at[slot2]
            z1, qs1, ks1 = z2_ref.at[slot1], qs2_ref.at[slot1], ks2_ref.at[slot1]
            blocks = iter(range(0, Z_WIDTH, PROJ_BLOCK))
            x1 = x_ref[0]
            xres2_ref[slot1] = x1
            norm(x1)
            heads = [mlstm_intra(h) for h in range(M_HEADS)]
            proj_u(0, u_new)
            proj_u(PROJ_BLOCK, u_new)
            gate_pre_acts = gate_pre()
            for st in heads:
                mlstm_products(st)
            proj_z(next(blocks), z1)
            proj_z(next(blocks), z1)
            for st in heads:
                mlstm_scan(st)
            proj_z(next(blocks), z1)
            gate_rows(gates2_ref.at[slot1], *gate_pre_acts)
            for p in range(A_PAIRS):
                band_write(p)
            for st in heads:
                mlstm_out(st)
                proj_z(next(blocks), z1)
            attention_all([functools.partial(proj_z, j, z1) for j in blocks])
            band_slide()
            for g, j in enumerate(range(0, D_MODEL, PROJ_BLOCK)):
                _merge_block(j, z, gmix, gatt, mix_ref, wbm_ref, wba_ref)
                conv(g, qs1, ks1)
            for g, j in enumerate(range(0, D_MODEL, PROJ_BLOCK)):
                _out_block(j, xres, mix_ref, wout_ref, y_ref.at[0])
                conv(D_MODEL // PROJ_BLOCK + g, qs1, ks1)

        for parity in (0, 1):
            pl.when(i % 2 == parity)(functools.partial(pipelined_step, 1 - parity, parity))
        conv_tail(t1, i + 1 < n_steps)
    else:
        u_ref[0:SUBLANES, :] = conv0_ref[0]
        u_ref[SUBLANES:SUBLANES + tile, :] = u_in_ref[0]
        for g in range(CONV_GROUPS):
            conv(g, qs, ks)
        conv_tail(t2, True)
        heads = [mlstm_intra(h) for h in range(M_HEADS)]
        for st in heads:
            mlstm_products(st)
        for st in heads:
            mlstm_scan(st)
        for p in range(A_PAIRS):
            band_write(p)
        for st in heads:
            mlstm_out(st)
        attention_all([])

    @pl.when(t2 == n_tiles - 1)
    def _state_out():
        for h in range(M_HEADS):
            n_out_ref[0, h:h + 1, :] = nd_ref[h].T[0:1, :]
        m_out_ref[0] = m_ref[0:M_HEADS, :]


def _project_kernel(x_ref, ng_ref, wa_ref, wb_ref, b_ref, wgt_ref, bgt_ref, u_ref, z_ref, gates_ref, hb_ref,
                    *, rows, chunk):
    norm, proj_u, proj_z, gate_pre, gate_rows = _stage1_pieces(
        rows, chunk, hb_ref, ng_ref, wa_ref, wb_ref, b_ref, wgt_ref, bgt_ref)
    norm(x_ref[...])
    for j in range(0, QK_WIDTH, PROJ_BLOCK):
        proj_u(j, u_ref)
    for j in range(0, Z_WIDTH, PROJ_BLOCK):
        proj_z(j, z_ref)
    gate_rows(gates_ref, *gate_pre())


def _merge_kernel(gmix_ref, gatt_ref, z_ref, x_ref, wbm_ref, wba_ref, wout_ref, y_ref, mix_ref):
    for j in range(0, D_MODEL, PROJ_BLOCK):
        _merge_block(j, z_ref, gmix_ref, gatt_ref, mix_ref, wbm_ref, wba_ref)
    for j in range(0, D_MODEL, PROJ_BLOCK):
        _out_block(j, x_ref, mix_ref, wout_ref, y_ref)


def _const_spec(shape):
    zeros = (0,) * len(shape)
    return pl.BlockSpec(shape, lambda i: zeros, pipeline_mode=pl.Buffered(1))


def _full_spec(shape):
    zeros = (0,) * len(shape)
    return pl.BlockSpec(shape, lambda i: zeros)


def _per_stream_spec(shape, n_tiles=1):
    nd = len(shape)
    return pl.BlockSpec((1,) + tuple(shape[1:]), lambda i: (i // n_tiles,) + (0,) * (nd - 1))


def _state_out_shapes(n_streams, keep):
    f32 = jnp.float32
    return (
        jax.ShapeDtypeStruct((n_streams, M_HEADS, M_HEAD_DIM, M_HEAD_DIM), f32),
        jax.ShapeDtypeStruct((n_streams, M_HEADS, LANES), f32),
        jax.ShapeDtypeStruct((n_streams, M_HEADS, LANES), f32),
        jax.ShapeDtypeStruct((n_streams, SUBLANES, QK_WIDTH), f32),
        jax.ShapeDtypeStruct((n_streams, keep, A_WIDTH), f32),
        jax.ShapeDtypeStruct((n_streams, keep, A_WIDTH), f32),
    )


def _compiler_params():
    return pltpu.CompilerParams(dimension_semantics=("arbitrary",), vmem_limit_bytes=VMEM_LIMIT_BYTES)


def _fresh_layer_call(x, stage1_params, mixer_params, merge_params, *, tile, chunk):
    n_streams, seq, _ = x.shape
    n_tiles = seq // tile
    n_steps = n_streams * n_tiles
    assert n_tiles > 1 and WINDOW % tile == 0 and seq % WINDOW == 0

    def tile_spec(step_of):
        return pl.BlockSpec((1, tile, D_MODEL), lambda i: (step_of(i) // n_tiles, step_of(i) % n_tiles, 0))

    params = tuple(stage1_params) + tuple(mixer_params) + tuple(merge_params)
    in_specs = [tile_spec(lambda i: jnp.minimum(i + 1, n_steps - 1)),
                pl.BlockSpec((1, tile, D_MODEL), lambda i: (0, 0, 0), pipeline_mode=pl.Buffered(1))]
    in_specs += [_const_spec(a.shape) for a in params]
    out_shape = (jax.ShapeDtypeStruct(x.shape, jnp.float32),) + _state_out_shapes(n_streams, WINDOW)
    out_specs = (tile_spec(lambda i: i),) + tuple(_per_stream_spec(s.shape, n_tiles) for s in out_shape[1:])
    f32, bf16 = jnp.float32, jnp.bfloat16
    scratch = [
        pltpu.VMEM((tile, D_MODEL), bf16),
        pltpu.VMEM((tile + SUBLANES, QK_WIDTH), f32),
        pltpu.VMEM((2, tile, Z_WIDTH), f32),
        pltpu.VMEM((2, tile, M_WIDTH), f32),
        pltpu.VMEM((2, tile, M_WIDTH), f32),
        pltpu.VMEM((2, 2 * M_HEADS, tile), f32),
        pltpu.VMEM((2, tile, D_MODEL), f32),
        pltpu.VMEM((WINDOW + tile, A_WIDTH), bf16),
        pltpu.VMEM((WINDOW + tile, A_WIDTH), bf16),
        pltpu.VMEM((tile, M_WIDTH), bf16),
        pltpu.VMEM((tile, A_WIDTH), bf16),
        pltpu.VMEM((tile, D_MODEL), bf16),
        pltpu.VMEM((M_HEADS, M_HEAD_DIM, M_HEAD_DIM), f32),
        pltpu.VMEM((SUBLANES, LANES), f32),
    ]
    kern = functools.partial(_layer_kernel, tile=tile, chunk=chunk, n_tiles=n_tiles, n_steps=n_steps,
                             pipelined=True)
    return pl.pallas_call(
        kern, grid=(n_steps,), in_specs=in_specs, out_specs=out_specs, out_shape=out_shape,
        scratch_shapes=scratch, compiler_params=_compiler_params(), name="layer_fresh",
    )(x, x, *params)


def _running_layer_call(x, state, stage1_params, mixer_params, merge_params):
    n_streams, seq, _ = x.shape
    rows = n_streams * seq
    f32, bf16 = jnp.float32, jnp.bfloat16
    x_rows = x.reshape(rows, D_MODEL)

    u, z, gate_rows = pl.pallas_call(
        functools.partial(_project_kernel, rows=rows, chunk=seq),
        grid=(1,),
        in_specs=[_const_spec(x_rows.shape)] + [_const_spec(a.shape) for a in stage1_params],
        out_specs=(_full_spec((rows, QK_WIDTH)), _full_spec((rows, Z_WIDTH)), _full_spec((2 * M_HEADS, rows))),
        out_shape=(jax.ShapeDtypeStruct((rows, QK_WIDTH), f32), jax.ShapeDtypeStruct((rows, Z_WIDTH), f32),
                   jax.ShapeDtypeStruct((2 * M_HEADS, rows), f32)),
        scratch_shapes=[pltpu.VMEM((rows, D_MODEL), bf16)],
        compiler_params=_compiler_params(), name="running_project",
    )(x_rows, *stage1_params)

    per_stream_in = (u.reshape(n_streams, seq, QK_WIDTH), z.reshape(n_streams, seq, Z_WIDTH),
                     gate_rows.reshape(2 * M_HEADS, n_streams, seq).transpose(1, 0, 2)) + tuple(state)
    out_shape = (jax.ShapeDtypeStruct((n_streams, seq, M_WIDTH), bf16),
                 jax.ShapeDtypeStruct((n_streams, seq, A_WIDTH), bf16)) + _state_out_shapes(n_streams, seq)
    scratch = [
        pltpu.VMEM((seq + SUBLANES, QK_WIDTH), f32),
        pltpu.VMEM((seq, M_WIDTH), f32),
        pltpu.VMEM((seq, M_WIDTH), f32),
        pltpu.VMEM((WINDOW + seq, A_WIDTH), bf16),
        pltpu.VMEM((WINDOW + seq, A_WIDTH), bf16),
        pltpu.VMEM((M_HEADS, M_HEAD_DIM, M_HEAD_DIM), f32),
        pltpu.VMEM((SUBLANES, LANES), f32),
    ]
    outs = pl.pallas_call(
        functools.partial(_layer_kernel, tile=seq, chunk=seq, n_tiles=1, n_steps=n_streams, pipelined=False),
        grid=(n_streams,),
        in_specs=[_per_stream_spec(a.shape) for a in per_stream_in] + [_const_spec(a.shape) for a in mixer_params],
        out_specs=tuple(_per_stream_spec(s.shape) for s in out_shape),
        out_shape=out_shape, scratch_shapes=scratch,
        compiler_params=_compiler_params(), name="running_mixers",
    )(*per_stream_in, *mixer_params)
    gmix, gatt = outs[0].reshape(rows, M_WIDTH), outs[1].reshape(rows, A_WIDTH)

    merge_in = (gmix, gatt, z, x_rows) + tuple(merge_params)
    y = pl.pallas_call(
        _merge_kernel, grid=(1,),
        in_specs=[_const_spec(a.shape) for a in merge_in],
        out_specs=_full_spec((rows, D_MODEL)),
        out_shape=jax.ShapeDtypeStruct((rows, D_MODEL), f32),
        scratch_shapes=[pltpu.VMEM((rows, D_MODEL), bf16)],
        compiler_params=_compiler_params(), name="running_merge",
    )(*merge_in)
    return (y.reshape(x.shape),) + tuple(outs[2:])


def _pair_bias(rel_bias, chunk):
    assert chunk - 1 <= REL_CLIP <= WINDOW and chunk + REL_CLIP <= 2 * REL_CLIP + 1
    width = WINDOW + chunk - BIAS_FAR
    n_ext = width + chunk - 1
    rev = rel_bias[:, ::-1].astype(jnp.float32)
    ext = jnp.concatenate([jnp.broadcast_to(rev[:, :1], (A_HEADS, chunk - 1)), rev[:, :width]], axis=1)
    ext = ext - rev[:, :1]
    flat = jnp.tile(ext, (1, chunk))[:, chunk - 1:chunk - 1 + chunk * (n_ext - 1)]
    bias = flat.reshape(A_HEADS, chunk, n_ext - 1)[:, :, :width]
    return bias.reshape(A_PAIRS, 2 * chunk, width)


def _layer_params(norm_g, w_in, b_in, conv_w, conv_b, m_head_g, q_norm_g, k_norm_g, w_bm, w_ba, w_out):
    gate_lo = W_A_WIDTH
    gate_hi = gate_lo + 2 * M_HEADS
    w_a = w_in[:, :gate_lo].astype(jnp.bfloat16)
    w_b = w_in[:, gate_hi:].astype(jnp.bfloat16)
    b_main = jnp.concatenate([b_in[:gate_lo], b_in[gate_hi:]])[None, :]
    w_gate_t = w_in[:, gate_lo:gate_hi].T.astype(jnp.bfloat16)
    stage1 = (norm_g[None, :], w_a, w_b, b_main, w_gate_t, b_in[gate_lo:gate_hi, None])
    mixers = (conv_w, conv_b[None, :], m_head_g, jnp.tile(q_norm_g, 2)[None, :], jnp.tile(k_norm_g, 2)[None, :])
    merge = (w_bm.astype(jnp.bfloat16), w_ba.astype(jnp.bfloat16), w_out.astype(jnp.bfloat16))
    return stage1, mixers, merge


PROMPT_TILE = 256


def kernel(x_prompt, x_sample, state_mlstm_C, state_mlstm_n, state_mlstm_m, state_mlstm_conv,
           cache_attn_k, cache_attn_v, norm_g, w_in, b_in, conv_w, conv_b, m_head_g,
           q_norm_g, k_norm_g, rel_bias, w_bm, w_ba, w_out):
    assert tuple(int(w) for w in IN_WIDTHS[:5]) == (M_WIDTH,) * 5 and sum(IN_WIDTHS) == w_in.shape[-1]
    depth = w_in.shape[0]
    xp, xs = x_prompt, x_sample
    n_p = xp.shape[0]
    n_s, t_s, _ = xs.shape

    def unpack(c, n, m, conv, k, v):
        n_streams = c.shape[0]
        return (c, n, m[:, :, 0], conv[:, SUBLANES - (CONV_WIDTH - 1):],
                k.reshape(n_streams, -1, A_HEADS, A_HEAD_DIM), v.reshape(n_streams, -1, A_HEADS, A_HEAD_DIM))

    outs_p, outs_s = [], []
    for l in range(depth):
        stage1, mixers, merge = _layer_params(norm_g[l], w_in[l], b_in[l], conv_w[l], conv_b[l], m_head_g[l],
                                              q_norm_g[l], k_norm_g[l], w_bm[l], w_ba[l], w_out[l])
        xp, *state_p = _fresh_layer_call(xp, stage1, mixers + (_pair_bias(rel_bias[l], CHUNK),), merge,
                                         tile=PROMPT_TILE, chunk=CHUNK)
        outs_p.append(unpack(*state_p))
        state = (state_mlstm_C[l], state_mlstm_n[l],
                 jnp.broadcast_to(state_mlstm_m[l][:, :, None], (n_s, M_HEADS, LANES)),
                 jnp.pad(state_mlstm_conv[l], ((0, 0), (SUBLANES - (CONV_WIDTH - 1), 0), (0, 0))),
                 cache_attn_k[l].reshape(n_s, -1, A_WIDTH), cache_attn_v[l].reshape(n_s, -1, A_WIDTH))
        xs, *state_s = _running_layer_call(xs, state, stage1, mixers + (_pair_bias(rel_bias[l], t_s),), merge)
        outs_s.append(unpack(*state_s))
    stack = lambda outs, i: jnp.stack([o[i] for o in outs])
    return (xp, xs) + tuple(stack(outs_p, i) for i in range(6)) + tuple(stack(outs_s, i) for i in range(6))
```
